```python
import math
import jax
import jax.numpy as jnp
from jax import lax
import numpy as np

D_MODEL = 2048
BATCH = 16
SEQ = 2048
DEPTH = 2
DEC_BATCH = 32
DEC_SEQ = 32
PAST_LEN = 4096

CHUNK = 64
Q_BLOCK = 128
N_ATT_LAYERS = (DEPTH + 1) // 2
N_SSM_LAYERS = DEPTH // 2
H_FOX = 8
DH_FOX = D_MODEL // 16
W_FOX = H_FOX * DH_FOX
H_DIFF = 8
DK_DIFF = D_MODEL // 32
DV_DIFF = 2 * DK_DIFF
W_DIFF = H_DIFF * DV_DIFF
D_SSM = D_MODEL
GROUP_CH = 16
N_GROUPS = D_SSM // GROUP_CH
P_STATE = 64

EPS = 1e-6
NEG_INF = -1e30
FORGET_BIAS_INIT = 5.0
ALIBI_MAX_EXP = 8.0
ATT_SECTIONS = (W_FOX, W_FOX, W_FOX, H_FOX, W_FOX, H_DIFF * 2 * DK_DIFF, H_DIFF * 2 * DK_DIFF, W_DIFF, W_DIFF)
ATT_IN_COLS = 4 * W_FOX + H_FOX + 2 * H_DIFF * 2 * DK_DIFF + 2 * W_DIFF

kernel_name = 'hybrid_fox_diff_s5_stream_step'


def _rmsnorm(x, g):
    x32 = x.astype(jnp.float32)
    y = x32 * lax.rsqrt(jnp.mean(x32 * x32, axis=-1, keepdims=True) + EPS)
    return (y * g.astype(jnp.float32)).astype(x.dtype)


def _adaln(x, c, g, w_mod, b_mod):
    mod = jax.nn.silu(c) @ w_mod + b_mod
    shift, scale, gate = jnp.split(mod, 3, axis=-1)
    h = _rmsnorm(x, g) * (1.0 + scale[:, None, :]) + shift[:, None, :]
    return h.astype(x.dtype), gate[:, None, :]


def _sweep_query_blocks(fn, q_leaves, axes):
    L = q_leaves[0].shape[axes[0]]
    if L <= Q_BLOCK or L % Q_BLOCK != 0:
        return fn(*q_leaves)
    nb = L // Q_BLOCK
    blocked = tuple(
        jnp.moveaxis(a.reshape(a.shape[:ax] + (nb, Q_BLOCK) + a.shape[ax + 1:]), ax, 0)
        for a, ax in zip(q_leaves, axes))
    out = lax.map(lambda args: fn(*args), blocked)
    out = jnp.moveaxis(out, 0, 1)
    return out.reshape((out.shape[0], L) + out.shape[3:])


def _fox_attention(q, k_new, v_new, logf_new, k_past, v_past, logf_past):
    P = k_past.shape[1]
    T = q.shape[1]
    k = jnp.concatenate([k_past, k_new], axis=1)
    v = jnp.concatenate([v_past, v_new], axis=1)
    F = jnp.cumsum(jnp.concatenate([logf_past.astype(jnp.float32), logf_new.astype(jnp.float32)], axis=1), axis=1)
    Fk = jnp.transpose(F, (0, 2, 1))
    Fq = F[:, P:]
    k_pos = jnp.arange(P + T)
    q_pos = P + jnp.arange(T)
    scale = DH_FOX ** -0.5

    def block(qb, Fqb, qpb):
        s = jnp.einsum('bqhd,bshd->bhqs', qb, k).astype(jnp.float32) * scale
        s = s + jnp.transpose(Fqb, (0, 2, 1))[..., None] - Fk[:, :, None, :]
        s = jnp.where(k_pos[None, :] <= qpb[:, None], s, NEG_INF)
        p = jax.nn.softmax(s, axis=-1)
        return jnp.einsum('bhqs,bshd->bqhd', p.astype(v.dtype), v)

    return _sweep_query_blocks(block, (q, Fq, q_pos), (1, 1, 0))


def _diff_attention(q, k_new, v_new, k_past, v_past, lam, subln_g, lambda_init):
    P = k_past.shape[1]
    T = q.shape[1]
    k = jnp.concatenate([k_past, k_new], axis=1)
    v = jnp.concatenate([v_past, v_new], axis=1)
    k1 = k[..., :DK_DIFF]
    k2 = k[..., DK_DIFF:]
    k_pos = jnp.arange(P + T)
    q_pos = P + jnp.arange(T)
    k_chunk = k_pos // CHUNK
    slopes = jnp.power(2.0, -ALIBI_MAX_EXP * jnp.arange(1, H_DIFF + 1, dtype=jnp.float32) / H_DIFF)
    scale = DK_DIFF ** -0.5

    def block(qb, qpb):
        dist = jnp.abs(qpb[:, None] - k_pos[None, :]).astype(jnp.float32)
        bias = -slopes[:, None, None] * dist[None]
        visible = k_chunk[None, :] <= (qpb // CHUNK)[:, None]

        def probs(qh, kh):
            s = jnp.einsum('bqhd,bshd->bhqs', qh, kh).astype(jnp.float32) * scale + bias
            s = jnp.where(visible, s, NEG_INF)
            return jax.nn.softmax(s, axis=-1)

        p = probs(qb[..., :DK_DIFF], k1) - lam * probs(qb[..., DK_DIFF:], k2)
        o = jnp.einsum('bhqs,bshd->bqhd', p.astype(v.dtype), v)
        return _rmsnorm(o, subln_g) * (1.0 - lambda_init)

    return _sweep_query_blocks(block, (q, q_pos), (1, 0))


def _attention_layer(x, c, l, ia, fk_past, fv_past, fl_past, dk_past, dv_past, p):
    B, T, _ = x.shape
    h, gate = _adaln(x, c, p['norm_g'][l], p['w_mod'][l], p['b_mod'][l])
    proj = h @ p['w_in_att'][ia]
    points = np.cumsum(ATT_SECTIONS)[:-1].tolist()
    q_f, k_f, v_f, f_logit, z_f, q_d, k_d, v_d, z_d = jnp.split(proj, points, axis=-1)
    logf = jax.nn.log_sigmoid(f_logit.astype(jnp.float32) + p['b_forget'][ia])
    q_f = q_f.reshape(B, T, H_FOX, DH_FOX)
    k_f = k_f.reshape(B, T, H_FOX, DH_FOX)
    v_f = v_f.reshape(B, T, H_FOX, DH_FOX)
    q_d = q_d.reshape(B, T, H_DIFF, 2 * DK_DIFF)
    k_d = k_d.reshape(B, T, H_DIFF, 2 * DK_DIFF)
    v_d = v_d.reshape(B, T, H_DIFF, DV_DIFF)
    lambda_init = 0.8 - 0.6 * math.exp(-0.3 * l)
    lam = (jnp.exp(jnp.sum(p['diff_lq1'][ia] * p['diff_lk1'][ia]))
           - jnp.exp(jnp.sum(p['diff_lq2'][ia] * p['diff_lk2'][ia])) + lambda_init).astype(jnp.float32)
    o_f = _fox_attention(q_f, k_f, v_f, logf, fk_past, fv_past, fl_past).reshape(B, T, W_FOX)
    o_d = _diff_attention(q_d, k_d, v_d, dk_past, dv_past, lam, p['diff_subln_g'][ia], lambda_init).reshape(B, T, W_DIFF)
    mixed = jnp.concatenate([o_f * jax.nn.silu(z_f), o_d * jax.nn.silu(z_d)], axis=-1)
    x = x + gate * (mixed @ p['w_out_att'][ia])
    return x, (k_f, v_f, logf, k_d, v_d)


def _s5_discretise(a_re, a_im, b_re, b_im, log_dt):
    dt = jnp.exp(log_dt.astype(jnp.float32))[:, None]
    mag = jnp.exp(dt * a_re)
    abar_re = mag * jnp.cos(dt * a_im)
    abar_im = mag * jnp.sin(dt * a_im)
    den = a_re * a_re + a_im * a_im
    nr = abar_re - 1.0
    ni = abar_im
    fr = (nr * a_re + ni * a_im) / den
    fi = (ni * a_re - nr * a_im) / den
    bb_re = fr[..., None] * b_re - fi[..., None] * b_im
    bb_im = fr[..., None] * b_im + fi[..., None] * b_re
    return abar_re, abar_im, bb_re, bb_im


def _s5_chunk(carry, u, abar_re, abar_im, bb_re, bb_im, c_re, c_im):
    T = u.shape[1]
    bu_re = jnp.einsum('btgc,gpc->btgp', u, bb_re)
    bu_im = jnp.einsum('btgc,gpc->btgp', u, bb_im)
    a_re = jnp.broadcast_to(abar_re[None, None], (1, T) + abar_re.shape)
    a_im = jnp.broadcast_to(abar_im[None, None], (1, T) + abar_im.shape)

    def combine(e_i, e_j):
        ar_i, ai_i, br_i, bi_i = e_i
        ar_j, ai_j, br_j, bi_j = e_j
        return (ar_j * ar_i - ai_j * ai_i,
                ar_j * ai_i + ai_j * ar_i,
                ar_j * br_i - ai_j * bi_i + br_j,
                ar_j * bi_i + ai_j * br_i + bi_j)

    ac_re, ac_im, x_re, x_im = lax.associative_scan(combine, (a_re, a_im, bu_re, bu_im), axis=1)
    cr, ci = carry
    x_re = x_re + ac_re * cr[:, None] - ac_im * ci[:, None]
    x_im = x_im + ac_re * ci[:, None] + ac_im * cr[:, None]
    y = jnp.einsum('btgp,gcp->btgc', x_re, c_re) - jnp.einsum('btgp,gcp->btgc', x_im, c_im)
    return (x_re[:, -1], x_im[:, -1]), y


def _s5_sequence(u, carry, abar_re, abar_im, bb_re, bb_im, c_re, c_im):
    B, L = u.shape[:2]
    if L > CHUNK and L % CHUNK == 0:
        nc = L // CHUNK
        uc = jnp.transpose(u.reshape(B, nc, CHUNK, N_GROUPS, GROUP_CH), (1, 0, 2, 3, 4))

        def step(cr, ub):
            return _s5_chunk(cr, ub, abar_re, abar_im, bb_re, bb_im, c_re, c_im)

        carry, ys = lax.scan(step, carry, uc)
        y = jnp.transpose(ys, (1, 0, 2, 3, 4)).reshape(B, L, N_GROUPS, GROUP_CH)
        return y, carry
    carry, y = _s5_chunk(carry, u, abar_re, abar_im, bb_re, bb_im, c_re, c_im)
    return y, carry


def _ssm_layer(x, c, l, isd, s_re, s_im, p):
    B, T, _ = x.shape
    h, gate = _adaln(x, c, p['norm_g'][l], p['w_mod'][l], p['b_mod'][l])
    u, z = jnp.split(h @ p['w_in_ssm'][isd], 2, axis=-1)
    u32 = u.astype(jnp.float32)
    abar_re, abar_im, bb_re, bb_im = _s5_discretise(p['ssm_a_re'][isd], p['ssm_a_im'][isd],
                                                    p['ssm_b_re'][isd], p['ssm_b_im'][isd], p['ssm_log_dt'][isd])
    y, (n_re, n_im) = _s5_sequence(u32.reshape(B, T, N_GROUPS, GROUP_CH),
                                   (s_re.astype(jnp.float32), s_im.astype(jnp.float32)),
                                   abar_re, abar_im, bb_re, bb_im, p['ssm_c_re'][isd], p['ssm_c_im'][isd])
    y = y.reshape(B, T, D_SSM) + p['ssm_d'][isd] * u32
    g = jax.nn.gelu(y)
    glu = g * jax.nn.sigmoid(g @ p['w_glu'][isd] + p['b_glu'][isd])
    out = (glu.astype(x.dtype) * jax.nn.silu(z)) @ p['w_out_ssm'][isd]
    x = x + gate * out
    return x, (n_re, n_im)


def _trunk(x, c, fk, fv, fl, dk, dv, s_re, s_im, p):
    att_rows = []
    ssm_states = []
    for l in range(DEPTH):
        if l % 2 == 0:
            ia = l // 2
            x, rows = _attention_layer(x, c, l, ia, fk[ia], fv[ia], fl[ia], dk[ia], dv[ia], p)
            att_rows.append(rows)
        else:
            isd = l // 2
            x, st = _ssm_layer(x, c, l, isd, s_re[isd], s_im[isd], p)
            ssm_states.append(st)
    y = _rmsnorm(x, p['final_norm_g'])
    att_out = tuple(jnp.stack([r[i] for r in att_rows]) for i in range(5))
    ssm_out = tuple(jnp.stack([s[i] for s in ssm_states]) for i in range(2))
    return y, att_out, ssm_out


def setup_inputs(seed: int = 0) -> dict:
    key = jax.random.key(seed)
    ks = jax.random.split(key, 40)
    f32 = jnp.float32

    def nrm(k, shape, s=1.0):
        return s * jax.random.normal(k, shape, f32)

    att_cache = (N_ATT_LAYERS, DEC_BATCH, PAST_LEN)
    ssm_state = (N_SSM_LAYERS, DEC_BATCH, N_GROUPS, P_STATE)
    ssm_p = (N_SSM_LAYERS, N_GROUPS, P_STATE)
    return {
        'x_prompt': nrm(ks[0], (BATCH, SEQ, D_MODEL)),
        'x_sample': nrm(ks[1], (DEC_BATCH, DEC_SEQ, D_MODEL)),
        'cache_fox_k': nrm(ks[2], att_cache + (H_FOX, DH_FOX)),
        'cache_fox_v': nrm(ks[3], att_cache + (H_FOX, DH_FOX)),
        'cache_fox_logf': jax.nn.log_sigmoid(FORGET_BIAS_INIT + nrm(ks[4], att_cache + (H_FOX,))),
        'cache_diff_k': nrm(ks[5], att_cache + (H_DIFF, 2 * DK_DIFF)),
        'cache_diff_v': nrm(ks[6], att_cache + (H_DIFF, DV_DIFF)),
        'state_ssm_re': nrm(ks[7], ssm_state, 0.3),
        'state_ssm_im': nrm(ks[8], ssm_state, 0.3),
        'c_prompt': nrm(ks[9], (BATCH, D_MODEL)),
        'c_sample': nrm(ks[10], (DEC_BATCH, D_MODEL)),
        'norm_g': 1.0 + nrm(ks[11], (DEPTH, D_MODEL), 0.01),
        'w_mod': nrm(ks[12], (DEPTH, D_MODEL, 3 * D_MODEL), D_MODEL ** -0.5),
        'b_mod': nrm(ks[13], (DEPTH, 3 * D_MODEL), 0.01),
        'w_in_att': nrm(ks[14], (N_ATT_LAYERS, D_MODEL, ATT_IN_COLS), D_MODEL ** -0.5),
        'b_forget': FORGET_BIAS_INIT + nrm(ks[15], (N_ATT_LAYERS, H_FOX), 0.5),
        'w_out_att': nrm(ks[16], (N_ATT_LAYERS, W_FOX + W_DIFF, D_MODEL), (W_FOX + W_DIFF) ** -0.5),
        'diff_lq1': nrm(ks[17], (N_ATT_LAYERS, DK_DIFF), 0.1),
        'diff_lk1': nrm(ks[18], (N_ATT_LAYERS, DK_DIFF), 0.1),
        'diff_lq2': nrm(ks[19], (N_ATT_LAYERS, DK_DIFF), 0.1),
        'diff_lk2': nrm(ks[20], (N_ATT_LAYERS, DK_DIFF), 0.1),
        'diff_subln_g': 1.0 + nrm(ks[21], (N_ATT_LAYERS, DV_DIFF), 0.01),
        'w_in_ssm': nrm(ks[22], (N_SSM_LAYERS, D_MODEL, 2 * D_SSM), D_MODEL ** -0.5),
        'ssm_a_re': -0.5 + nrm(ks[23], ssm_p, 0.01),
        'ssm_a_im': math.pi * jnp.arange(P_STATE, dtype=f32) + nrm(ks[24], ssm_p, 0.01),
        'ssm_b_re': nrm(ks[25], ssm_p + (GROUP_CH,), (2 * GROUP_CH) ** -0.5),
        'ssm_b_im': nrm(ks[26], ssm_p + (GROUP_CH,), (2 * GROUP_CH) ** -0.5),
        'ssm_c_re': nrm(ks[27], (N_SSM_LAYERS, N_GROUPS, GROUP_CH, P_STATE), P_STATE ** -0.5),
        'ssm_c_im': nrm(ks[28], (N_SSM_LAYERS, N_GROUPS, GROUP_CH, P_STATE), P_STATE ** -0.5),
        'ssm_d': nrm(ks[29], (N_SSM_LAYERS, D_SSM)),
        'ssm_log_dt': jax.random.uniform(ks[30], (N_SSM_LAYERS, N_GROUPS), f32, math.log(0.001), math.log(0.1)),
        'w_glu': nrm(ks[31], (N_SSM_LAYERS, D_SSM, D_SSM), D_SSM ** -0.5),
        'b_glu': nrm(ks[32], (N_SSM_LAYERS, D_SSM), 0.01),
        'w_out_ssm': nrm(ks[33], (N_SSM_LAYERS, D_SSM, D_MODEL), D_SSM ** -0.5),
        'final_norm_g': 1.0 + nrm(ks[34], (D_MODEL,), 0.01),
    }


def reference(x_prompt, x_sample, cache_fox_k, cache_fox_v, cache_fox_logf, cache_diff_k, cache_diff_v,
              state_ssm_re, state_ssm_im, c_prompt, c_sample, norm_g, w_mod, b_mod, w_in_att, b_forget,
              w_out_att, diff_lq1, diff_lk1, diff_lq2, diff_lk2, diff_subln_g, w_in_ssm, ssm_a_re, ssm_a_im,
              ssm_b_re, ssm_b_im, ssm_c_re, ssm_c_im, ssm_d, ssm_log_dt, w_glu, b_glu, w_out_ssm, final_norm_g):
    p = dict(norm_g=norm_g, w_mod=w_mod, b_mod=b_mod, w_in_att=w_in_att, b_forget=b_forget,
             w_out_att=w_out_att, diff_lq1=diff_lq1, diff_lk1=diff_lk1, diff_lq2=diff_lq2, diff_lk2=diff_lk2,
             diff_subln_g=diff_subln_g, w_in_ssm=w_in_ssm, ssm_a_re=ssm_a_re, ssm_a_im=ssm_a_im,
             ssm_b_re=ssm_b_re, ssm_b_im=ssm_b_im, ssm_c_re=ssm_c_re, ssm_c_im=ssm_c_im, ssm_d=ssm_d,
             ssm_log_dt=ssm_log_dt, w_glu=w_glu, b_glu=b_glu, w_out_ssm=w_out_ssm, final_norm_g=final_norm_g)
    B = x_prompt.shape[0]
    dt = x_prompt.dtype
    e_fk = jnp.zeros((N_ATT_LAYERS, B, 0, H_FOX, DH_FOX), dt)
    e_fl = jnp.zeros((N_ATT_LAYERS, B, 0, H_FOX), jnp.float32)
    e_dk = jnp.zeros((N_ATT_LAYERS, B, 0, H_DIFF, 2 * DK_DIFF), dt)
    e_dv = jnp.zeros((N_ATT_LAYERS, B, 0, H_DIFF, DV_DIFF), dt)
    z_s = jnp.zeros((N_SSM_LAYERS, B, N_GROUPS, P_STATE), jnp.float32)
    y_prompt, (fk_p, fv_p, fl_p, dk_p, dv_p), (sre_p, sim_p) = _trunk(
        x_prompt, c_prompt, e_fk, e_fk, e_fl, e_dk, e_dv, z_s, z_s, p)
    y_sample, (fk_s, fv_s, fl_s, dk_s, dv_s), (sre_s, sim_s) = _trunk(
        x_sample, c_sample, cache_fox_k, cache_fox_v, cache_fox_logf, cache_diff_k, cache_diff_v,
        state_ssm_re, state_ssm_im, p)
    return (y_prompt, y_sample, fk_p, fv_p, fl_p, dk_p, dv_p, sre_p, sim_p,
            fk_s, fv_s, fl_s, dk_s, dv_s, sre_s, sim_s)
```

```python
import functools
import math

import numpy as np
import jax
import jax.numpy as jnp
from jax import lax
from jax.experimental import pallas as pl
from jax.experimental.pallas import tpu as pltpu

F32 = jnp.float32
BF16 = jnp.bfloat16

EPS = 1e-6
NEG_INF = -1e30
CHUNK = 64
ALIBI_MAX_EXP = 8.0
N_HEADS = 8
HEAD_DIM = 128
GROUP_CH = 16
P_STATE = 64
LANES = 128
GROUPS_PER_BLOCK = LANES // GROUP_CH
STATE_HALF = GROUPS_PER_BLOCK * P_STATE
VMEM_LIMIT_BYTES = 56 * 1024 * 1024
MM_ROWS = 1024
ATT_TQ = 512
ATT_TK_CACHE = 1024


def _cparams(n_axes):
    return pltpu.CompilerParams(dimension_semantics=("arbitrary",) * n_axes,
                                vmem_limit_bytes=VMEM_LIMIT_BYTES)


def _silu(x):
    return x * jax.nn.sigmoid(x)


def _log_sigmoid(x):
    return jnp.minimum(x, 0.0) - jnp.log(1.0 + jnp.exp(-jnp.abs(x)))


def _mod_kernel(c_ref, w_ref, b_ref, o_ref):
    a = _silu(c_ref[...]).astype(BF16)
    acc = jnp.dot(a, w_ref[0].astype(BF16), preferred_element_type=F32)
    o_ref[0] = acc + b_ref[0]


def _mod_call(c_all, w_mod, b_mod, tn=512):
    depth, d, n = w_mod.shape
    r = c_all.shape[0]
    return pl.pallas_call(
        _mod_kernel,
        grid=(depth, n // tn),
        in_specs=[pl.BlockSpec((r, d), lambda l, j: (0, 0)),
                  pl.BlockSpec((1, d, tn), lambda l, j: (l, 0, j)),
                  pl.BlockSpec((1, 1, tn), lambda l, j: (l, 0, j))],
        out_specs=pl.BlockSpec((1, r, tn), lambda l, j: (l, 0, j)),
        out_shape=jax.ShapeDtypeStruct((depth, r, n), F32),
        compiler_params=_cparams(2),
        name="adaln_mod",
    )(c_all, w_mod, b_mod.reshape(depth, 1, n))


def _adaln_kernel(x_ref, shift_ref, scale_ref, g_ref, h_ref):
    x = x_ref[...]
    y = x * lax.rsqrt(jnp.mean(x * x, axis=-1, keepdims=True) + EPS) * g_ref[...]
    h_ref[...] = (y * (1.0 + scale_ref[...]) + shift_ref[...]).astype(h_ref.dtype)


def _row_tiles(b, t, rows):
    if t >= rows:
        return 1, rows
    return min(b, rows // t), t


def _adaln_call(x, shift, scale, g):
    b, t, d = x.shape
    bb, tt = _row_tiles(b, t, 512)
    return pl.pallas_call(
        _adaln_kernel,
        grid=(b // bb, t // tt),
        in_specs=[pl.BlockSpec((bb, tt, d), lambda i, j: (i, j, 0)),
                  pl.BlockSpec((bb, 1, d), lambda i, j: (i, 0, 0)),
                  pl.BlockSpec((bb, 1, d), lambda i, j: (i, 0, 0)),
                  pl.BlockSpec((1, d), lambda i, j: (0, 0))],
        out_specs=pl.BlockSpec((bb, tt, d), lambda i, j: (i, j, 0)),
        out_shape=jax.ShapeDtypeStruct((b, t, d), BF16),
        compiler_params=_cparams(2),
        name="adaln_norm",
    )(x, shift, scale, g.reshape(1, d))


def _mm_kernel(*refs, n_pairs, epilogue):
    o_ref = refs[-1]
    extra = refs[2 * n_pairs:-1]
    acc = None
    for p in range(n_pairs):
        a_ref, w_ref = refs[2 * p], refs[2 * p + 1]
        a = a_ref[...].reshape(-1, a_ref.shape[-1])
        part = jnp.dot(a, w_ref[...], preferred_element_type=F32)
        acc = part if acc is None else acc + part
    acc = acc.reshape(o_ref.shape)
    if epilogue is not None:
        acc = epilogue(acc, *[e[...] for e in extra])
    o_ref[...] = acc.astype(o_ref.dtype)


def _mm(pairs, out_dtype, n_out, *, b, t, tn, rows=None, a_tb=False, out_tb=False,
        epilogue=None, extras=(), name="matmul"):
    bb, tt = _row_tiles(b, t, MM_ROWS if rows is None else rows)
    if a_tb or out_tb:
        assert bb == 1
    nj = n_out // tn
    in_specs, args = [], []
    for a, w in pairs:
        k = w.shape[0]
        if a_tb:
            in_specs.append(pl.BlockSpec((tt, k), lambda ib, it, j: (it, ib)))
        else:
            in_specs.append(pl.BlockSpec((bb, tt, k), lambda ib, it, j: (ib, it, 0)))
        in_specs.append(pl.BlockSpec((k, tn), lambda ib, it, j: (0, j)))
        args += [a, w]
    for arr, kind in extras:
        if kind == "row":
            in_specs.append(pl.BlockSpec((1, tn), lambda ib, it, j: (0, j)))
        elif kind == "batch_row":
            in_specs.append(pl.BlockSpec((bb, 1, tn), lambda ib, it, j: (ib, 0, j)))
        elif kind == "tile":
            in_specs.append(pl.BlockSpec((bb, tt, tn), lambda ib, it, j: (ib, it, j)))
        elif kind == "tile_tb":
            in_specs.append(pl.BlockSpec((tt, tn), lambda ib, it, j: (it, ib * nj + j)))
        else:
            raise ValueError(kind)
        args.append(arr)
    if out_tb:
        out_spec = pl.BlockSpec((tt, tn), lambda ib, it, j: (it, ib * nj + j))
        out_shape = jax.ShapeDtypeStruct((t, b * n_out), out_dtype)
    else:
        out_spec = pl.BlockSpec((bb, tt, tn), lambda ib, it, j: (ib, it, j))
        out_shape = jax.ShapeDtypeStruct((b, t, n_out), out_dtype)
    return pl.pallas_call(
        functools.partial(_mm_kernel, n_pairs=len(pairs), epilogue=epilogue),
        grid=(b // bb, t // tt, nj),
        in_specs=in_specs,
        out_specs=out_spec,
        out_shape=out_shape,
        compiler_params=_cparams(3),
        name=name,
    )(*args)


def _ep_logf(acc, bias):
    return _log_sigmoid(acc + bias)


def _ep_residual(acc, x, gate):
    return x + gate * acc


def _ep_glu(acc, g, bias, z):
    g = g.astype(F32)
    return g * jax.nn.sigmoid(acc + bias) * _silu(z.astype(F32))


def _ep_residual_norm(acc, x, gate, fg):
    x2 = x + gate * acc
    return x2 * lax.rsqrt(jnp.mean(x2 * x2, axis=-1, keepdims=True) + EPS) * fg


def _cumsum_kernel(x_ref, o_ref):
    x = x_ref[0]
    n = x.shape[-1]
    lane = lax.broadcasted_iota(jnp.int32, x.shape, 1)
    d = 1
    while d < n:
        x = x + jnp.where(lane >= d, pltpu.roll(x, d, axis=1), 0.0)
        d *= 2
    o_ref[0] = x


def _cumsum_call(lf):
    b, nh, n = lf.shape
    return pl.pallas_call(
        _cumsum_kernel,
        grid=(b,),
        in_specs=[pl.BlockSpec((1, nh, n), lambda i: (i, 0, 0))],
        out_specs=pl.BlockSpec((1, nh, n), lambda i: (i, 0, 0)),
        out_shape=jax.ShapeDtypeStruct((b, nh, n), F32),
        compiler_params=_cparams(1),
        name="logf_cumsum",
    )(lf)


_NT = (((1,), (1,)), ((), ()))


def _softmax_step(s, v, m, l, acc):
    m_new = jnp.maximum(m, jnp.max(s, axis=-1, keepdims=True))
    alpha = jnp.exp(m - m_new)
    p = jnp.exp(s - m_new)
    l_new = alpha * l + jnp.sum(p, axis=-1, keepdims=True)
    acc_new = alpha * acc + jnp.dot(p.astype(BF16), v, preferred_element_type=F32)
    return m_new, l_new, acc_new


def _ref_softmax_step(s, v, m_ref, l_ref, acc_ref):
    m, l, acc = _softmax_step(s, v, m_ref[...], l_ref[...], acc_ref[...])
    m_ref[...] = m
    l_ref[...] = l
    acc_ref[...] = acc


def _diag_positions(tq, tk):
    r = lax.broadcasted_iota(jnp.int32, (tq, tk), 0)
    c = lax.broadcasted_iota(jnp.int32, (tq, tk), 1)
    return r, c


def _lambda_full(lq1, lk1, lq2, lk2, lambda_init):
    s1 = jnp.sum(lq1 * lk1, axis=-1, keepdims=True)
    s2 = jnp.sum(lq2 * lk2, axis=-1, keepdims=True)
    return jnp.exp(s1) - jnp.exp(s2) + lambda_init


def _split_q_diff(q, scale):
    qf = q.astype(F32) * scale
    lane = lax.broadcasted_iota(jnp.int32, qf.shape, 1)
    first = lane < (qf.shape[-1] // 2)
    return jnp.where(first, qf, 0.0).astype(BF16), jnp.where(first, 0.0, qf).astype(BF16)


def _diff_finish(acc1, l1, acc2, l2, lam, subln_g, lambda_init, z):
    o = acc1 / l1 - lam * (acc2 / l2)
    o = o * lax.rsqrt(jnp.mean(o * o, axis=-1, keepdims=True) + EPS) * subln_g
    o = o * (1.0 - lambda_init)
    return o * _silu(z.astype(F32))


def _fox_self_kernel(q_ref, k_ref, v_ref, f_ref, z_ref, o_ref, kb_ref, vb_ref, m_ref, l_ref, acc_ref, *, tq):
    qi = pl.program_id(2)

    @pl.when(qi == 0)
    def _():
        kb_ref[...] = k_ref[0].astype(BF16)
        vb_ref[...] = v_ref[0].astype(BF16)

    q = (q_ref[0].astype(F32) * (HEAD_DIM ** -0.5)).astype(BF16)
    m_ref[...] = jnp.full(m_ref.shape, NEG_INF, F32)
    l_ref[...] = jnp.zeros(l_ref.shape, F32)
    acc_ref[...] = jnp.zeros(acc_ref.shape, F32)

    def logits(j):
        start = pl.multiple_of(j * tq, tq)
        kb = kb_ref[pl.ds(start, tq), :]
        s = lax.dot_general(q, kb, _NT, preferred_element_type=F32)
        return s - f_ref[0, pl.ds(j, 1), :], vb_ref[pl.ds(start, tq), :]

    def full_block(j, carry):
        s, vb = logits(j)
        _ref_softmax_step(s, vb, m_ref, l_ref, acc_ref)
        return carry

    lax.fori_loop(0, qi, full_block, 0)
    s, vb = logits(qi)
    r, c = _diag_positions(tq, tq)
    _ref_softmax_step(jnp.where(c <= r, s, NEG_INF), vb, m_ref, l_ref, acc_ref)
    o = acc_ref[...] / l_ref[...]
    o_ref[0] = (o * _silu(z_ref[0].astype(F32))).astype(o_ref.dtype)


def _fox_self_call(qz, kf, vf, frow):
    b, t, _ = qz.shape
    tq = min(ATT_TQ, t)
    nh = N_HEADS
    frow = frow.reshape(b * nh, t // tq, tq)
    return pl.pallas_call(
        functools.partial(_fox_self_kernel, tq=tq),
        grid=(b, nh, t // tq),
        in_specs=[pl.BlockSpec((1, tq, HEAD_DIM), lambda ib, h, i: (ib, i, h)),
                  pl.BlockSpec((1, t, HEAD_DIM), lambda ib, h, i: (ib, 0, h)),
                  pl.BlockSpec((1, t, HEAD_DIM), lambda ib, h, i: (ib, 0, h)),
                  pl.BlockSpec((1, t // tq, tq), lambda ib, h, i: (ib * nh + h, 0, 0)),
                  pl.BlockSpec((1, tq, HEAD_DIM), lambda ib, h, i: (ib, i, nh + h))],
        out_specs=pl.BlockSpec((1, tq, HEAD_DIM), lambda ib, h, i: (ib, i, h)),
        out_shape=jax.ShapeDtypeStruct((b, t, nh * HEAD_DIM), BF16),
        scratch_shapes=[pltpu.VMEM((t, HEAD_DIM), BF16), pltpu.VMEM((t, HEAD_DIM), BF16),
                        pltpu.VMEM((tq, 1), F32), pltpu.VMEM((tq, 1), F32), pltpu.VMEM((tq, HEAD_DIM), F32)],
        compiler_params=_cparams(3),
        name="fox_attention_self",
    )(qz, kf, vf, frow, qz)


def _diff_self_kernel(q_ref, k_ref, v_ref, z_ref, slope_ref, lq1_ref, lk1_ref, lq2_ref, lk2_ref, g_ref, o_ref,
                      kb_ref, vb_ref, m1_ref, l1_ref, a1_ref, m2_ref, l2_ref, a2_ref, *, tq, lambda_init):
    qi = pl.program_id(2)

    @pl.when(qi == 0)
    def _():
        kb_ref[...] = k_ref[0].astype(BF16)
        vb_ref[...] = v_ref[0].astype(BF16)

    q1, q2 = _split_q_diff(q_ref[0], (HEAD_DIM // 2) ** -0.5)
    slope = slope_ref[0][:, 0:1]
    for m_ref, l_ref, a_ref in ((m1_ref, l1_ref, a1_ref), (m2_ref, l2_ref, a2_ref)):
        m_ref[...] = jnp.full(m_ref.shape, NEG_INF, F32)
        l_ref[...] = jnp.zeros(l_ref.shape, F32)
        a_ref[...] = jnp.zeros(a_ref.shape, F32)

    def blocks(j):
        start = pl.multiple_of(j * tq, tq)
        return kb_ref[pl.ds(start, tq), :], vb_ref[pl.ds(start, tq), :]

    def full_block(j, carry):
        kb, vb = blocks(j)
        col = lax.broadcasted_iota(jnp.int32, (1, tq), 1) + (j - qi) * tq
        bias = slope * col.astype(F32)
        s1 = lax.dot_general(q1, kb, _NT, preferred_element_type=F32) + bias
        _ref_softmax_step(s1, vb, m1_ref, l1_ref, a1_ref)
        s2 = lax.dot_general(q2, kb, _NT, preferred_element_type=F32) + bias
        _ref_softmax_step(s2, vb, m2_ref, l2_ref, a2_ref)
        return carry

    lax.fori_loop(0, qi, full_block, 0)
    kb, vb = blocks(qi)
    r, c = _diag_positions(tq, tq)
    visible = (c // CHUNK) <= (r // CHUNK)
    bias = slope * (r - jnp.abs(r - c)).astype(F32)
    s1 = lax.dot_general(q1, kb, _NT, preferred_element_type=F32) + bias
    _ref_softmax_step(jnp.where(visible, s1, NEG_INF), vb, m1_ref, l1_ref, a1_ref)
    s2 = lax.dot_general(q2, kb, _NT, preferred_element_type=F32) + bias
    _ref_softmax_step(jnp.where(visible, s2, NEG_INF), vb, m2_ref, l2_ref, a2_ref)
    lam = _lambda_full(lq1_ref[...], lk1_ref[...], lq2_ref[...], lk2_ref[...], lambda_init)
    o = _diff_finish(a1_ref[...], l1_ref[...], a2_ref[...], l2_ref[...], lam, g_ref[...], lambda_init, z_ref[0])
    o_ref[0] = o.astype(o_ref.dtype)


def _alibi_slopes():
    s = np.power(2.0, -ALIBI_MAX_EXP * np.arange(1, N_HEADS + 1, dtype=np.float64) / N_HEADS)
    return jnp.asarray(np.broadcast_to(s[:, None, None], (N_HEADS, 1, LANES)), F32)


def _diff_self_call(qz, kd, vd, lam_params, subln_g, lambda_init):
    b, t, _ = qz.shape
    tq = min(ATT_TQ, t)
    nh = N_HEADS
    small = lambda a: pl.BlockSpec(a.shape, lambda ib, h, i: (0,) * a.ndim)
    lam_params = [p.reshape(1, -1) for p in lam_params]
    g = subln_g.reshape(1, -1)
    stats = [pltpu.VMEM((tq, 1), F32), pltpu.VMEM((tq, 1), F32), pltpu.VMEM((tq, HEAD_DIM), F32)]
    return pl.pallas_call(
        functools.partial(_diff_self_kernel, tq=tq, lambda_init=lambda_init),
        grid=(b, nh, t // tq),
        in_specs=[pl.BlockSpec((1, tq, HEAD_DIM), lambda ib, h, i: (ib, i, 2 * nh + h)),
                  pl.BlockSpec((1, t, HEAD_DIM), lambda ib, h, i: (ib, 0, h)),
                  pl.BlockSpec((1, t, HEAD_DIM), lambda ib, h, i: (ib, 0, h)),
                  pl.BlockSpec((1, tq, HEAD_DIM), lambda ib, h, i: (ib, i, 3 * nh + h)),
                  pl.BlockSpec((1, 1, LANES), lambda ib, h, i: (h, 0, 0))]
                 + [small(p) for p in lam_params] + [small(g)],
        out_specs=pl.BlockSpec((1, tq, HEAD_DIM), lambda ib, h, i: (ib, i, h)),
        out_shape=jax.ShapeDtypeStruct((b, t, nh * HEAD_DIM), BF16),
        scratch_shapes=[pltpu.VMEM((t, HEAD_DIM), BF16), pltpu.VMEM((t, HEAD_DIM), BF16)] + stats + stats,
        compiler_params=_cparams(3),
        name="diff_attention_self",
    )(qz, kd, vd, qz, _alibi_slopes(), *lam_params, g)


def _init_stats(t):
    return jnp.full((t, 1), NEG_INF, F32), jnp.zeros((t, 1), F32), jnp.zeros((t, HEAD_DIM), F32)


def _fox_cached_kernel(q_ref, kp_ref, vp_ref, kn_ref, vn_ref, f_ref, z_ref, o_ref, *, tk):
    t = q_ref.shape[1]
    p_len = kp_ref.shape[1]
    q = (q_ref[0].astype(F32) * (HEAD_DIM ** -0.5)).astype(BF16)
    m, l, acc = _init_stats(t)
    for j in range(p_len // tk):
        kb = kp_ref[0, j * tk:(j + 1) * tk, :].astype(BF16)
        vb = vp_ref[0, j * tk:(j + 1) * tk, :].astype(BF16)
        s = lax.dot_general(q, kb, _NT, preferred_element_type=F32) - f_ref[0, :, j * tk:(j + 1) * tk]
        m, l, acc = _softmax_step(s, vb, m, l, acc)
    s = lax.dot_general(q, kn_ref[0].astype(BF16), _NT, preferred_element_type=F32) - f_ref[0, :, p_len:p_len + t]
    r, c = _diag_positions(t, t)
    m, l, acc = _softmax_step(jnp.where(c <= r, s, NEG_INF), vn_ref[0].astype(BF16), m, l, acc)
    o_ref[0] = (acc / l * _silu(z_ref[0].astype(F32))).astype(o_ref.dtype)


def _cache_specs(t, p_len, q_col, z_col):
    nh = N_HEADS
    return [pl.BlockSpec((1, t, HEAD_DIM), lambda ib, h: (ib, 0, q_col + h)),
            pl.BlockSpec((1, p_len, HEAD_DIM), lambda ib, h: (ib, 0, h)),
            pl.BlockSpec((1, p_len, HEAD_DIM), lambda ib, h: (ib, 0, h)),
            pl.BlockSpec((1, t, HEAD_DIM), lambda ib, h: (ib, 0, h)),
            pl.BlockSpec((1, t, HEAD_DIM), lambda ib, h: (ib, 0, h))], \
        pl.BlockSpec((1, t, HEAD_DIM), lambda ib, h: (ib, 0, z_col + h))


def _fox_cached_call(qz, k_past, v_past, kf, vf, frow):
    b, t, _ = qz.shape
    p_len = k_past.shape[1]
    tk = min(ATT_TK_CACHE, p_len)
    nh = N_HEADS
    qkv_specs, z_spec = _cache_specs(t, p_len, 0, nh)
    return pl.pallas_call(
        functools.partial(_fox_cached_kernel, tk=tk),
        grid=(b, nh),
        in_specs=qkv_specs + [pl.BlockSpec((1, 1, frow.shape[-1]), lambda ib, h: (ib * nh + h, 0, 0)), z_spec],
        out_specs=pl.BlockSpec((1, t, HEAD_DIM), lambda ib, h: (ib, 0, h)),
        out_shape=jax.ShapeDtypeStruct((b, t, nh * HEAD_DIM), BF16),
        compiler_params=_cparams(2),
        name="fox_attention_cached",
    )(qz, k_past, v_past, kf, vf, frow, qz)


def _diff_cached_kernel(q_ref, kp_ref, vp_ref, kn_ref, vn_ref, z_ref, slope_ref, lq1_ref, lk1_ref, lq2_ref, lk2_ref,
                        g_ref, o_ref, *, tk, lambda_init):
    t = q_ref.shape[1]
    p_len = kp_ref.shape[1]
    q1, q2 = _split_q_diff(q_ref[0], (HEAD_DIM // 2) ** -0.5)
    slope = slope_ref[0][:, 0:1]
    m1, l1, a1 = _init_stats(t)
    m2, l2, a2 = _init_stats(t)
    for j in range(p_len // tk):
        kb = kp_ref[0, j * tk:(j + 1) * tk, :].astype(BF16)
        vb = vp_ref[0, j * tk:(j + 1) * tk, :].astype(BF16)
        col = lax.broadcasted_iota(jnp.int32, (1, tk), 1) + (j * tk - p_len)
        bias = slope * col.astype(F32)
        s1 = lax.dot_general(q1, kb, _NT, preferred_element_type=F32) + bias
        m1, l1, a1 = _softmax_step(s1, vb, m1, l1, a1)
        s2 = lax.dot_general(q2, kb, _NT, preferred_element_type=F32) + bias
        m2, l2, a2 = _softmax_step(s2, vb, m2, l2, a2)
    kb = kn_ref[0].astype(BF16)
    vb = vn_ref[0].astype(BF16)
    r, c = _diag_positions(t, t)
    visible = ((c + p_len) // CHUNK) <= ((r + p_len) // CHUNK)
    bias = slope * (r - jnp.abs(r - c)).astype(F32)
    s1 = lax.dot_general(q1, kb, _NT, preferred_element_type=F32) + bias
    m1, l1, a1 = _softmax_step(jnp.where(visible, s1, NEG_INF), vb, m1, l1, a1)
    s2 = lax.dot_general(q2, kb, _NT, preferred_element_type=F32) + bias
    m2, l2, a2 = _softmax_step(jnp.where(visible, s2, NEG_INF), vb, m2, l2, a2)
    lam = _lambda_full(lq1_ref[...], lk1_ref[...], lq2_ref[...], lk2_ref[...], lambda_init)
    o_ref[0] = _diff_finish(a1, l1, a2, l2, lam, g_ref[...], lambda_init, z_ref[0]).astype(o_ref.dtype)


def _diff_cached_call(qz, k_past, v_past, kd, vd, lam_params, subln_g, lambda_init):
    b, t, _ = qz.shape
    p_len = k_past.shape[1]
    tk = min(ATT_TK_CACHE, p_len)
    nh = N_HEADS
    qkv_specs, z_spec = _cache_specs(t, p_len, 2 * nh, 3 * nh)
    small = lambda a: pl.BlockSpec(a.shape, lambda ib, h: (0,) * a.ndim)
    lam_params = [p.reshape(1, -1) for p in lam_params]
    g = subln_g.reshape(1, -1)
    return pl.pallas_call(
        functools.partial(_diff_cached_kernel, tk=tk, lambda_init=lambda_init),
        grid=(b, nh),
        in_specs=qkv_specs + [z_spec, pl.BlockSpec((1, 1, LANES), lambda ib, h: (h, 0, 0))]
                 + [small(p) for p in lam_params] + [small(g)],
        out_specs=pl.BlockSpec((1, t, HEAD_DIM), lambda ib, h: (ib, 0, h)),
        out_shape=jax.ShapeDtypeStruct((b, t, nh * HEAD_DIM), BF16),
        compiler_params=_cparams(2),
        name="diff_attention_cached",
    )(qz, k_past, v_past, kd, vd, qz, _alibi_slopes(), *lam_params, g)


def _s5_disc_kernel(are_ref, aim_ref, bre_ref, bim_ref, ldt_ref, abr_ref, abi_ref, bbr_ref, bbi_ref):
    a_re, a_im = are_ref[...], aim_ref[...]
    dt = jnp.exp(ldt_ref[...])
    mag = jnp.exp(dt * a_re)
    abar_re = mag * jnp.cos(dt * a_im)
    abar_im = mag * jnp.sin(dt * a_im)
    den = a_re * a_re + a_im * a_im
    nr = abar_re - 1.0
    ni = abar_im
    fr = (nr * a_re + ni * a_im) / den
    fi = (ni * a_re - nr * a_im) / den
    abr_ref[...] = abar_re
    abi_ref[...] = abar_im
    bbr_ref[...] = fr * bre_ref[...] - fi * bim_ref[...]
    bbi_ref[...] = fr * bim_ref[...] + fi * bre_ref[...]


def _s5_disc_call(a_re, a_im, b_re, b_im, log_dt):
    g, p, gc = b_re.shape
    flat = lambda a: jnp.broadcast_to(a, (g, p, gc)).reshape(-1, LANES)
    args = [flat(a_re[..., None]), flat(a_im[..., None]), flat(b_re), flat(b_im), flat(log_dt[:, None, None])]
    shape = jax.ShapeDtypeStruct(args[0].shape, F32)
    outs = pl.pallas_call(_s5_disc_kernel, out_shape=[shape] * 4, name="s5_discretise")(*args)
    abar_re, abar_im, bb_re, bb_im = [o.reshape(g, p, gc) for o in outs]
    return abar_re[..., 0], abar_im[..., 0], bb_re, bb_im


def _s5_scan_kernel(u_ref, bblk_ref, cblk_ref, ar_ref, ai_ref, d_ref, x0_ref, g_ref, st_ref, bu_ref, xs_ref,
                    *, nb, lc):
    @pl.when(pl.program_id(1) == 0)
    def _():
        st_ref[...] = x0_ref[...]

    u = u_ref[...]
    bu_ref[...] = jnp.dot(u, bblk_ref[0], preferred_element_type=F32)
    hs = STATE_HALF
    ar = jnp.broadcast_to(ar_ref[0], (nb, hs))
    ai = jnp.broadcast_to(ai_ref[0], (nb, hs))

    def step(t, carry):
        xr, xi = carry
        r0 = pl.multiple_of(t * nb, nb)
        nr = ar * xr - ai * xi + bu_ref[pl.ds(r0, nb), 0:hs]
        ni = ar * xi + ai * xr + bu_ref[pl.ds(r0, nb), hs:2 * hs]
        xs_ref[pl.ds(r0, nb), 0:hs] = nr.astype(BF16)
        xs_ref[pl.ds(r0, nb), hs:2 * hs] = ni.astype(BF16)
        return nr, ni

    xr, xi = lax.fori_loop(0, lc, step, (st_ref[:, 0:hs], st_ref[:, hs:2 * hs]))
    st_ref[:, 0:hs] = xr
    st_ref[:, hs:2 * hs] = xi
    y = jnp.dot(xs_ref[...], cblk_ref[0], preferred_element_type=F32) + d_ref[0] * u.astype(F32)
    g_ref[...] = jax.nn.gelu(y).astype(g_ref.dtype)


def _s5_scan_call(u2d, bblk, cblk, abar_re, abar_im, d_skip, x0, *, nb, lc):
    rows, d = u2d.shape
    nf = d // LANES
    hs = STATE_HALF
    chunk_rows = lc * nb
    return pl.pallas_call(
        functools.partial(_s5_scan_kernel, nb=nb, lc=lc),
        grid=(nf, rows // chunk_rows),
        in_specs=[pl.BlockSpec((chunk_rows, LANES), lambda f, c: (c, f)),
                  pl.BlockSpec((1, LANES, 2 * hs), lambda f, c: (f, 0, 0)),
                  pl.BlockSpec((1, 2 * hs, LANES), lambda f, c: (f, 0, 0)),
                  pl.BlockSpec((1, 1, hs), lambda f, c: (f, 0, 0)),
                  pl.BlockSpec((1, 1, hs), lambda f, c: (f, 0, 0)),
                  pl.BlockSpec((1, 1, LANES), lambda f, c: (f, 0, 0)),
                  pl.BlockSpec((nb, 2 * hs), lambda f, c: (0, f))],
        out_specs=[pl.BlockSpec((chunk_rows, LANES), lambda f, c: (c, f)),
                   pl.BlockSpec((nb, 2 * hs), lambda f, c: (0, f))],
        out_shape=[jax.ShapeDtypeStruct((rows, d), BF16), jax.ShapeDtypeStruct((nb, nf * 2 * hs), F32)],
        scratch_shapes=[pltpu.VMEM((chunk_rows, 2 * hs), F32), pltpu.VMEM((chunk_rows, 2 * hs), BF16)],
        compiler_params=_cparams(2),
        name="s5_scan",
    )(u2d, bblk, cblk, abar_re, abar_im, d_skip, x0)


def _s5_block_weights(abar_re, abar_im, bb_re, bb_im, c_re, c_im, d_skip):
    g = bb_re.shape[0]
    nf, gb = g // GROUPS_PER_BLOCK, GROUPS_PER_BLOCK
    eye = jnp.eye(gb, dtype=F32)
    b4 = lambda b: jnp.einsum("fgpc,gh->fgchp", b.reshape(nf, gb, P_STATE, GROUP_CH), eye).reshape(nf, LANES, STATE_HALF)
    c4 = lambda c: jnp.einsum("fgcp,gh->fhpgc", c.reshape(nf, gb, GROUP_CH, P_STATE), eye).reshape(nf, STATE_HALF, LANES)
    bblk = jnp.concatenate([b4(bb_re), b4(bb_im)], axis=-1).astype(BF16)
    cblk = jnp.concatenate([c4(c_re), -c4(c_im)], axis=1).astype(BF16)
    row = lambda a: a.reshape(nf, 1, STATE_HALF)
    return bblk, cblk, row(abar_re), row(abar_im), d_skip.reshape(nf, 1, LANES)


def _pack_state(s_re, s_im):
    b, g, p = s_re.shape
    nf = g // GROUPS_PER_BLOCK
    return jnp.concatenate([s_re.reshape(b, nf, STATE_HALF), s_im.reshape(b, nf, STATE_HALF)], axis=-1).reshape(b, -1)


def _unpack_state(st, g):
    b = st.shape[0]
    nf = g // GROUPS_PER_BLOCK
    st = st.reshape(b, nf, 2, GROUPS_PER_BLOCK, P_STATE)
    return st[:, :, 0].reshape(b, g, P_STATE), st[:, :, 1].reshape(b, g, P_STATE)


def _trunk(x, mods, cache, state, w):
    b, t, d = x.shape
    nh = N_HEADS

    shift, scale, gate = mods[0]
    h0 = _adaln_call(x, shift, scale, w["norm_g"][0])
    proj = functools.partial(_mm, b=b, t=t, tn=1024)
    qz = proj([(h0, w["w_qz"])], BF16, 4 * nh * HEAD_DIM, name="in_att_qz")
    kf = proj([(h0, w["w_kf"])], F32, nh * HEAD_DIM, name="in_att_kf")
    vf = proj([(h0, w["w_vf"])], F32, nh * HEAD_DIM, name="in_att_vf")
    kd = proj([(h0, w["w_kd"])], F32, nh * HEAD_DIM, name="in_att_kd")
    vd = proj([(h0, w["w_vd"])], F32, nh * HEAD_DIM, name="in_att_vd")
    logf = _mm([(h0, w["w_f"])], F32, LANES, b=b, t=t, tn=LANES, epilogue=_ep_logf,
               extras=[(w["b_forget"], "row")], name="in_att_logf")[..., :nh]
    lam_params = (w["diff_lq1"], w["diff_lk1"], w["diff_lq2"], w["diff_lk2"])
    lambda_init = 0.8 - 0.6 * math.exp(-0.3 * 0)
    if cache is None:
        frow = _cumsum_call(jnp.transpose(logf, (0, 2, 1))).reshape(b * nh, 1, t)
        o_f = _fox_self_call(qz, kf, vf, frow)
        o_d = _diff_self_call(qz, kd, vd, lam_params, w["diff_subln_g"], lambda_init)
    else:
        fk, fv, fl, dk, dv = cache
        p_len = fk.shape[1]
        lf = jnp.concatenate([fl.astype(F32), logf], axis=1)
        lf = jnp.pad(lf, ((0, 0), (0, (-lf.shape[1]) % LANES), (0, 0)))
        frow = _cumsum_call(jnp.transpose(lf, (0, 2, 1))).reshape(b * nh, 1, -1)
        o_f = _fox_cached_call(qz, fk.reshape(b, p_len, -1), fv.reshape(b, p_len, -1), kf, vf, frow)
        o_d = _diff_cached_call(qz, dk.reshape(b, p_len, -1), dv.reshape(b, p_len, -1), kd, vd, lam_params,
                                w["diff_subln_g"], lambda_init)
    x1 = _mm([(o_f, w["w_out_f"]), (o_d, w["w_out_d"])], F32, d, b=b, t=t, tn=1024, epilogue=_ep_residual,
             extras=[(x, "tile"), (gate, "batch_row")], name="out_att")

    shift, scale, gate = mods[1]
    h1 = _adaln_call(x1, shift, scale, w["norm_g"][1])
    z = _mm([(h1, w["w_z"])], BF16, d, b=b, t=t, tn=1024, name="in_ssm_z")
    token_major = t >= MM_ROWS
    if token_major:
        u2d = _mm([(h1, w["w_u"])], BF16, d, b=b, t=t, tn=1024, out_tb=True, name="in_ssm_u").reshape(t * b, d)
        lc = CHUNK
    else:
        u = _mm([(h1, w["w_u"])], BF16, d, b=b, t=t, tn=1024, name="in_ssm_u")
        u2d = jnp.transpose(u, (1, 0, 2)).reshape(t * b, d)
        lc = t
    x0 = jnp.zeros((b, w["n_groups"] * 2 * P_STATE), F32) if state is None else _pack_state(*state)
    g2d, st = _s5_scan_call(u2d, w["bblk"], w["cblk"], w["abar_re"], w["abar_im"], w["d_skip"], x0, nb=b, lc=lc)
    if token_major:
        g_tb = g2d.reshape(t, b * d)
        a2 = _mm([(g_tb, w["w_glu"])], BF16, d, b=b, t=t, tn=1024, a_tb=True, epilogue=_ep_glu,
                 extras=[(g_tb, "tile_tb"), (w["b_glu"], "row"), (z, "tile")], name="glu")
    else:
        g = jnp.transpose(g2d.reshape(t, b, d), (1, 0, 2))
        a2 = _mm([(g, w["w_glu"])], BF16, d, b=b, t=t, tn=1024, epilogue=_ep_glu,
                 extras=[(g, "tile"), (w["b_glu"], "row"), (z, "tile")], name="glu")
    y = _mm([(a2, w["w_out_ssm"])], F32, d, b=b, t=t, tn=d, rows=MM_ROWS // 2, epilogue=_ep_residual_norm,
            extras=[(x1, "tile"), (gate, "batch_row"), (w["final_norm_g"], "row")], name="out_ssm_norm")
    n_re, n_im = _unpack_state(st, w["n_groups"])
    lead = lambda a: a.reshape((1, b, t, nh, -1))
    return y, (lead(kf), lead(vf), logf.reshape(1, b, t, nh), lead(kd), lead(vd)), (n_re[None], n_im[None])


def kernel(x_prompt, x_sample, cache_fox_k, cache_fox_v, cache_fox_logf, cache_diff_k, cache_diff_v, state_ssm_re, state_ssm_im, c_prompt, c_sample, norm_g, w_mod, b_mod, w_in_att, b_forget, w_out_att, diff_lq1, diff_lk1, diff_lq2, diff_lk2, diff_subln_g, w_in_ssm, ssm_a_re, ssm_a_im, ssm_b_re, ssm_b_im, ssm_c_re, ssm_c_im, ssm_d, ssm_log_dt, w_glu, b_glu, w_out_ssm, final_norm_g):
    d = x_prompt.shape[-1]
    bp, bs = x_prompt.shape[0], x_sample.shape[0]
    nh = N_HEADS
    wf = nh * HEAD_DIM
    assert w_in_att.shape[0] == 1 and w_in_ssm.shape[0] == 1 and norm_g.shape[0] == 2
    assert wf == d // 2 and w_in_att.shape[-1] == 8 * wf + nh

    wa = w_in_att[0]
    offs = np.cumsum([0, wf, wf, wf, nh, wf, wf, wf, wf, wf])
    sec = lambda i: wa[:, offs[i]:offs[i + 1]]
    abar_re, abar_im, bb_re, bb_im = _s5_disc_call(ssm_a_re[0], ssm_a_im[0], ssm_b_re[0], ssm_b_im[0], ssm_log_dt[0])
    bblk, cblk, ar_row, ai_row, d_row = _s5_block_weights(abar_re, abar_im, bb_re, bb_im, ssm_c_re[0], ssm_c_im[0],
                                                        ssm_d[0])
    w = dict(
        norm_g=norm_g,
        w_qz=jnp.concatenate([sec(0), sec(4), sec(5), sec(8)], axis=1).astype(BF16),
        w_kf=sec(1).astype(BF16), w_vf=sec(2).astype(BF16), w_kd=sec(6).astype(BF16), w_vd=sec(7).astype(BF16),
        w_f=jnp.pad(sec(3), ((0, 0), (0, LANES - nh))).astype(BF16),
        b_forget=jnp.pad(b_forget, ((0, 0), (0, LANES - nh))),
        w_out_f=w_out_att[0, :wf].astype(BF16), w_out_d=w_out_att[0, wf:].astype(BF16),
        diff_lq1=diff_lq1, diff_lk1=diff_lk1, diff_lq2=diff_lq2, diff_lk2=diff_lk2, diff_subln_g=diff_subln_g,
        w_u=w_in_ssm[0, :, :d].astype(BF16), w_z=w_in_ssm[0, :, d:].astype(BF16),
        bblk=bblk, cblk=cblk, abar_re=ar_row, abar_im=ai_row, d_skip=d_row, n_groups=ssm_a_re.shape[1],
        w_glu=w_glu[0].astype(BF16), b_glu=b_glu, w_out_ssm=w_out_ssm[0].astype(BF16),
        final_norm_g=final_norm_g.reshape(1, d),
    )

    mod = _mod_call(jnp.concatenate([c_prompt, c_sample], axis=0), w_mod, b_mod)

    def mods(rows):
        return [tuple(mod[l, rows, i * d:(i + 1) * d][:, None, :] for i in range(3)) for l in range(2)]

    y_p, att_p, ssm_p = _trunk(x_prompt, mods(slice(0, bp)), None, None, w)
    cache = (cache_fox_k[0], cache_fox_v[0], cache_fox_logf[0], cache_diff_k[0], cache_diff_v[0])
    y_s, att_s, ssm_s = _trunk(x_sample, mods(slice(bp, bp + bs)), cache, (state_ssm_re[0], state_ssm_im[0]), w)
    return (y_p, y_s) + att_p + ssm_p + att_s + ssm_s
```

```python
import functools
import math

import numpy as np
import jax
import jax.numpy as jnp
from jax import lax
from jax.experimental import pallas as pl
from jax.experimental.pallas import tpu as pltpu

F32 = jnp.float32
BF16 = jnp.bfloat16

EPS = 1e-6
NEG_INF = -1e30
CHUNK = 64
ALIBI_MAX_EXP = 8.0
N_HEADS = 8
HEAD_DIM = 128
GROUP_CH = 16
P_STATE = 64
LANES = 128
SUBLANES = 8
BF16_ROWS = 16
GROUPS_PER_BLOCK = LANES // GROUP_CH
STATE_HALF = GROUPS_PER_BLOCK * P_STATE
VMEM_LIMIT_BYTES = 56 * 1024 * 1024
MM_ROWS = 1024
ATT_TQ = 512
ATT_SUB = 256
ATT_TK_CACHE = 512


def _cparams(n_axes):
    return pltpu.CompilerParams(dimension_semantics=("arbitrary",) * n_axes,
                                vmem_limit_bytes=VMEM_LIMIT_BYTES)


def _silu(x):
    return x * jax.nn.sigmoid(x)


def _log_sigmoid(x):
    return jnp.minimum(x, 0.0) - jnp.log(1.0 + jnp.exp(-jnp.abs(x)))


def _mod_kernel(c_ref, w_ref, b_ref, o_ref):
    a = _silu(c_ref[...]).astype(BF16)
    acc = jnp.dot(a, w_ref[0].astype(BF16), preferred_element_type=F32)
    o_ref[0] = acc + b_ref[0]


def _mod_call(c_all, w_mod, b_mod, tn=512):
    depth, d, n = w_mod.shape
    r = c_all.shape[0]
    return pl.pallas_call(
        _mod_kernel,
        grid=(depth, n // tn),
        in_specs=[pl.BlockSpec((r, d), lambda l, j: (0, 0)),
                  pl.BlockSpec((1, d, tn), lambda l, j: (l, 0, j)),
                  pl.BlockSpec((1, 1, tn), lambda l, j: (l, 0, j))],
        out_specs=pl.BlockSpec((1, r, tn), lambda l, j: (l, 0, j)),
        out_shape=jax.ShapeDtypeStruct((depth, r, n), F32),
        compiler_params=_cparams(2),
        name="adaln_mod",
    )(c_all, w_mod, b_mod.reshape(depth, 1, n))


def _adaln_kernel(x_ref, shift_ref, scale_ref, g_ref, h_ref):
    x = x_ref[...]
    y = x * lax.rsqrt(jnp.mean(x * x, axis=-1, keepdims=True) + EPS) * g_ref[...]
    h_ref[...] = (y * (1.0 + scale_ref[...]) + shift_ref[...]).astype(h_ref.dtype)


def _row_tiles(b, t, rows):
    if t >= rows:
        return 1, rows
    return min(b, rows // t), t


def _adaln_call(x, shift, scale, g):
    b, t, d = x.shape
    bb, tt = _row_tiles(b, t, MM_ROWS // 2)
    return pl.pallas_call(
        _adaln_kernel,
        grid=(b // bb, t // tt),
        in_specs=[pl.BlockSpec((bb, tt, d), lambda i, j: (i, j, 0)),
                  pl.BlockSpec((bb, 1, d), lambda i, j: (i, 0, 0)),
                  pl.BlockSpec((bb, 1, d), lambda i, j: (i, 0, 0)),
                  pl.BlockSpec((1, d), lambda i, j: (0, 0))],
        out_specs=pl.BlockSpec((bb, tt, d), lambda i, j: (i, j, 0)),
        out_shape=jax.ShapeDtypeStruct((b, t, d), BF16),
        compiler_params=_cparams(2),
        name="adaln_norm",
    )(x, shift, scale, g.reshape(1, d))


def _mm_kernel(*refs, n_pairs, epilogue):
    o_ref = refs[-1]
    extra = refs[2 * n_pairs:-1]
    acc = None
    for p in range(n_pairs):
        a_ref, w_ref = refs[2 * p], refs[2 * p + 1]
        a = a_ref[...].reshape(-1, a_ref.shape[-1])
        part = jnp.dot(a, w_ref[...], preferred_element_type=F32)
        acc = part if acc is None else acc + part
    acc = acc.reshape(o_ref.shape)
    if epilogue is not None:
        acc = epilogue(acc, *[e[...] for e in extra])
    o_ref[...] = acc.astype(o_ref.dtype)


def _mm(pairs, out_dtype, n_out, *, b, t, tn, rows=None, a_grouped=0, out_grouped=0,
        epilogue=None, extras=(), name="matmul"):
    bb, tt = _row_tiles(b, t, MM_ROWS if rows is None else rows)
    if a_grouped or out_grouped:
        assert bb == 1
    nj = n_out // tn
    in_specs, args = [], []
    for a, w in pairs:
        k = w.shape[0]
        if a_grouped:
            in_specs.append(pl.BlockSpec((tt // a_grouped, 1, a_grouped, k), lambda ib, it, j: (it, ib, 0, 0)))
        else:
            in_specs.append(pl.BlockSpec((bb, tt, k), lambda ib, it, j: (ib, it, 0)))
        in_specs.append(pl.BlockSpec((k, tn), lambda ib, it, j: (0, j)))
        args += [a, w]
    for arr, kind in extras:
        if kind == "row":
            in_specs.append(pl.BlockSpec((1, tn), lambda ib, it, j: (0, j)))
        elif kind == "batch_row":
            in_specs.append(pl.BlockSpec((bb, 1, tn), lambda ib, it, j: (ib, 0, j)))
        elif kind == "tile":
            in_specs.append(pl.BlockSpec((bb, tt, tn), lambda ib, it, j: (ib, it, j)))
        elif kind == "tile_grouped":
            in_specs.append(pl.BlockSpec((tt // a_grouped, 1, a_grouped, tn), lambda ib, it, j: (it, ib, 0, j)))
        else:
            raise ValueError(kind)
        args.append(arr)
    if out_grouped:
        out_spec = pl.BlockSpec((tt // out_grouped, 1, out_grouped, tn), lambda ib, it, j: (it, ib, 0, j))
        out_shape = jax.ShapeDtypeStruct((t // out_grouped, b, out_grouped, n_out), out_dtype)
    else:
        out_spec = pl.BlockSpec((bb, tt, tn), lambda ib, it, j: (ib, it, j))
        out_shape = jax.ShapeDtypeStruct((b, t, n_out), out_dtype)
    return pl.pallas_call(
        functools.partial(_mm_kernel, n_pairs=len(pairs), epilogue=epilogue),
        grid=(b // bb, t // tt, nj),
        in_specs=in_specs,
        out_specs=out_spec,
        out_shape=out_shape,
        compiler_params=_cparams(3),
        name=name,
    )(*args)


def _ep_logf(acc, bias):
    return _log_sigmoid(acc + bias)


def _ep_residual(acc, x, gate):
    return x + gate * acc


def _ep_glu(acc, g, bias, z):
    g = g.reshape(acc.shape).astype(F32)
    return g * jax.nn.sigmoid(acc + bias) * _silu(z.astype(F32))


def _ep_residual_norm(acc, x, gate, fg):
    x2 = x + gate * acc
    return x2 * lax.rsqrt(jnp.mean(x2 * x2, axis=-1, keepdims=True) + EPS) * fg


def _kv_kernel(a_ref, w_ref, o_ref, *maybe_ob):
    a = a_ref[...].reshape(-1, a_ref.shape[-1])
    acc = jnp.dot(a, w_ref[...], preferred_element_type=F32)
    rows = acc.shape[0]
    for h in range(N_HEADS):
        part = acc[:, h * HEAD_DIM:(h + 1) * HEAD_DIM]
        o_ref[pl.ds(h, rows, stride=N_HEADS), :] = part
        if maybe_ob:
            ob_ref = maybe_ob[0]
            ob_ref[:, h] = part.reshape(ob_ref.shape[0], ob_ref.shape[2], HEAD_DIM).astype(ob_ref.dtype)


def _kv_proj(a, w, *, with_bf16, name):
    b, t, k = a.shape
    bb, tt = _row_tiles(b, t, MM_ROWS)
    n = N_HEADS * HEAD_DIM
    nt = t // tt
    out_shape = [jax.ShapeDtypeStruct((b * t * N_HEADS, HEAD_DIM), F32)]
    out_specs = [pl.BlockSpec((bb * tt * N_HEADS, HEAD_DIM), lambda ib, it: (ib * nt + it, 0))]
    if with_bf16:
        out_shape.append(jax.ShapeDtypeStruct((b, N_HEADS, t, HEAD_DIM), BF16))
        out_specs.append(pl.BlockSpec((bb, N_HEADS, tt, HEAD_DIM), lambda ib, it: (ib, 0, it, 0)))
    outs = pl.pallas_call(
        _kv_kernel,
        grid=(b // bb, nt),
        in_specs=[pl.BlockSpec((bb, tt, k), lambda ib, it: (ib, it, 0)),
                  pl.BlockSpec((k, n), lambda ib, it: (0, 0))],
        out_specs=out_specs,
        out_shape=out_shape,
        compiler_params=_cparams(2),
        name=name,
    )(a, w)
    return outs if with_bf16 else (outs[0], None)


def _cumsum_kernel(x_ref, o_ref):
    x = x_ref[0]
    n = x.shape[-1]
    lane = lax.broadcasted_iota(jnp.int32, x.shape, 1)
    d = 1
    while d < n:
        x = x + jnp.where(lane >= d, pltpu.roll(x, d, axis=1), 0.0)
        d *= 2
    o_ref[0] = x


def _cumsum_call(lf):
    b, nh, n = lf.shape
    return pl.pallas_call(
        _cumsum_kernel,
        grid=(b,),
        in_specs=[pl.BlockSpec((1, nh, n), lambda i: (i, 0, 0))],
        out_specs=pl.BlockSpec((1, nh, n), lambda i: (i, 0, 0)),
        out_shape=jax.ShapeDtypeStruct((b, nh, n), F32),
        compiler_params=_cparams(1),
        name="logf_cumsum",
    )(lf)


_NT = (((1,), (1,)), ((), ()))


def _lane_fit(x, width):
    if width <= LANES:
        return x[:, 0:width]
    return jnp.concatenate([x] * (width // LANES), axis=1)


def _softmax_update(s, va, m_prev, acc_prev):
    m_new = jnp.maximum(m_prev, jnp.max(s, axis=1, keepdims=True))
    alpha = jnp.exp(m_prev - m_new)
    p = jnp.exp(s - _lane_fit(m_new, s.shape[1]))
    acc = _lane_fit(alpha, 2 * LANES) * acc_prev + jnp.dot(p.astype(BF16), va, preferred_element_type=F32)
    return m_new, acc


def _ref_update(rows, s, va, m_ref, acc_ref):
    m, acc = _softmax_update(s, va, m_ref[rows, :], acc_ref[rows, :])
    m_ref[rows, :] = m
    acc_ref[rows, :] = acc


def _normalised(acc):
    return acc[:, 0:HEAD_DIM] / acc[:, HEAD_DIM:2 * HEAD_DIM]


def _with_ones(v):
    return jnp.concatenate([v, jnp.ones(v.shape, v.dtype)], axis=1)


def _lambda_full(lq1, lk1, lq2, lk2, lambda_init):
    s1 = jnp.sum(lq1 * lk1, axis=-1, keepdims=True)
    s2 = jnp.sum(lq2 * lk2, axis=-1, keepdims=True)
    return jnp.exp(s1) - jnp.exp(s2) + lambda_init


def _split_q_diff(q):
    qf = q.astype(F32) * ((HEAD_DIM // 2) ** -0.5)
    lane = lax.broadcasted_iota(jnp.int32, qf.shape, 1)
    first = lane < (HEAD_DIM // 2)
    return jnp.where(first, qf, 0.0).astype(BF16), jnp.where(first, 0.0, qf).astype(BF16)


def _diff_finish(o1, o2, lam, subln_g, lambda_init, z):
    o = o1 - lam * o2
    o = o * lax.rsqrt(jnp.mean(o * o, axis=-1, keepdims=True) + EPS) * subln_g
    o = o * (1.0 - lambda_init)
    return o * _silu(z.astype(F32))


def _alibi_slopes():
    s = np.power(2.0, -ALIBI_MAX_EXP * np.arange(1, N_HEADS + 1, dtype=np.float64) / N_HEADS)
    return jnp.asarray(np.broadcast_to(s[:, None, None], (N_HEADS, 1, LANES)), F32)


def _diag_bias_and_mask(rr, cc, slope, offset):
    visible = ((cc + offset) // CHUNK) <= ((rr + offset) // CHUNK)
    return slope * (rr - jnp.abs(rr - cc)).astype(F32), visible


def _self_attention_kernel(*refs, tq, sub, n_maps, lambda_init):
    if n_maps == 1:
        q_ref, k_ref, v_ref, f_ref, z_ref, o_ref, va_ref, m1_ref, a1_ref = refs
    else:
        (q_ref, k_ref, v_ref, z_ref, slope_ref, lq1_ref, lk1_ref, lq2_ref, lk2_ref, g_ref, o_ref,
         va_ref, m1_ref, a1_ref, m2_ref, a2_ref) = refs
    qi = pl.program_id(2)
    t = k_ref.shape[2]

    @pl.when(qi == 0)
    def _():
        va_ref[:, 0:HEAD_DIM] = v_ref[0, 0]
        va_ref[:, HEAD_DIM:2 * HEAD_DIM] = jnp.ones((t, HEAD_DIM), BF16)

    if n_maps == 1:
        qs = ((q_ref[0].astype(F32) * (HEAD_DIM ** -0.5)).astype(BF16),)
        stats = ((m1_ref, a1_ref),)
    else:
        qs = _split_q_diff(q_ref[0])
        stats = ((m1_ref, a1_ref), (m2_ref, a2_ref))
        slope = slope_ref[0][:, 0:1]
    for m_ref, a_ref in stats:
        m_ref[...] = jnp.full(m_ref.shape, NEG_INF, F32)
        a_ref[...] = jnp.zeros(a_ref.shape, F32)
    nsub = tq // sub

    def full_block(j, q_tile):
        start = j * tq
        k = k_ref[0, 0, start:start + tq, :]
        va = va_ref[start:start + tq, :]
        if n_maps == 1:
            bias = -f_ref[0, j:j + 1, :]
        else:
            col = lax.broadcasted_iota(jnp.int32, (1, tq), 1) + (j - q_tile) * tq
            bias = slope * col.astype(F32)
        for r in range(nsub):
            rows = slice(r * sub, (r + 1) * sub)
            for q, (m_ref, a_ref) in zip(qs, stats):
                s = lax.dot_general(q[rows], k, _NT, preferred_element_type=F32) + bias
                _ref_update(rows, s, va, m_ref, a_ref)

    def diag_block(j):
        start = j * tq
        for r in range(nsub):
            rows = slice(r * sub, (r + 1) * sub)
            w = (r + 1) * sub
            k = k_ref[0, 0, start:start + w, :]
            va = va_ref[start:start + w, :]
            rr = lax.broadcasted_iota(jnp.int32, (sub, w), 0) + r * sub
            cc = lax.broadcasted_iota(jnp.int32, (sub, w), 1)
            if n_maps == 1:
                bias, visible = -f_ref[0, j:j + 1, 0:w], cc <= rr
            else:
                bias, visible = _diag_bias_and_mask(rr, cc, slope, 0)
            for q, (m_ref, a_ref) in zip(qs, stats):
                s = lax.dot_general(q[rows], k, _NT, preferred_element_type=F32) + bias
                _ref_update(rows, jnp.where(visible, s, NEG_INF), va, m_ref, a_ref)

    for q_tile in range(t // tq):
        @pl.when(qi == q_tile)
        def _(q_tile=q_tile):
            for j in range(q_tile):
                full_block(j, q_tile)
            diag_block(q_tile)

    if n_maps == 1:
        o = _normalised(a1_ref[...]) * _silu(z_ref[0].astype(F32))
    else:
        lam = _lambda_full(lq1_ref[...], lk1_ref[...], lq2_ref[...], lk2_ref[...], lambda_init)
        o = _diff_finish(_normalised(a1_ref[...]), _normalised(a2_ref[...]), lam, g_ref[...], lambda_init, z_ref[0])
    o_ref[0] = o.astype(o_ref.dtype)


def _self_attention_call(qz, kb, vb, *, q_col, z_col, frow=None, diff=None, name):
    b, t, _ = qz.shape
    tq = min(ATT_TQ, t)
    sub = min(ATT_SUB, tq)
    nh = N_HEADS
    in_specs = [pl.BlockSpec((1, tq, HEAD_DIM), lambda ib, h, i: (ib, i, q_col * nh + h)),
                pl.BlockSpec((1, 1, t, HEAD_DIM), lambda ib, h, i: (ib, h, 0, 0)),
                pl.BlockSpec((1, 1, t, HEAD_DIM), lambda ib, h, i: (ib, h, 0, 0))]
    z_spec = pl.BlockSpec((1, tq, HEAD_DIM), lambda ib, h, i: (ib, i, z_col * nh + h))
    stats = [pltpu.VMEM((tq, LANES), F32), pltpu.VMEM((tq, 2 * HEAD_DIM), F32)]
    if diff is None:
        n_maps, lambda_init = 1, 0.0
        in_specs += [pl.BlockSpec((1, t // tq, tq), lambda ib, h, i: (ib * nh + h, 0, 0)), z_spec]
        args = (qz, kb, vb, frow.reshape(b * nh, t // tq, tq), qz)
    else:
        n_maps = 2
        lam_params, subln_g, lambda_init = diff
        small = [p.reshape(1, -1) for p in lam_params] + [subln_g.reshape(1, -1)]
        in_specs += [z_spec, pl.BlockSpec((1, 1, LANES), lambda ib, h, i: (h, 0, 0))]
        in_specs += [pl.BlockSpec(p.shape, lambda ib, h, i: (0, 0)) for p in small]
        args = (qz, kb, vb, qz, _alibi_slopes(), *small)
    return pl.pallas_call(
        functools.partial(_self_attention_kernel, tq=tq, sub=sub, n_maps=n_maps, lambda_init=lambda_init),
        grid=(b, nh, t // tq),
        in_specs=in_specs,
        out_specs=pl.BlockSpec((1, tq, HEAD_DIM), lambda ib, h, i: (ib, i, h)),
        out_shape=jax.ShapeDtypeStruct((b, t, nh * HEAD_DIM), BF16),
        scratch_shapes=[pltpu.VMEM((t, 2 * HEAD_DIM), BF16)] + stats * n_maps,
        compiler_params=_cparams(3),
        name=name,
    )(*args)


def _cached_attention_kernel(*refs, ck, n_maps, lambda_init):
    if n_maps == 1:
        q_ref, kp_ref, vp_ref, kn_ref, vn_ref, f_ref, z_ref, o_ref, m_ref, a_ref = refs
    else:
        (q_ref, kp_ref, vp_ref, kn_ref, vn_ref, z_ref, slope_ref, lq1_ref, lk1_ref, lq2_ref, lk2_ref, g_ref,
         o_ref, m_ref, a_ref) = refs
    c = pl.program_id(1)
    nc = pl.num_programs(1)
    t = q_ref.shape[1]
    p_len = nc * ck
    nh = N_HEADS

    def head_queries(h):
        q = q_ref[0][:, h * HEAD_DIM:(h + 1) * HEAD_DIM]
        if n_maps == 1:
            return ((q.astype(F32) * (HEAD_DIM ** -0.5)).astype(BF16),)
        return _split_q_diff(q)

    def update(h, i, s, va):
        m, acc = _softmax_update(s, va, m_ref[i, h], a_ref[i, h])
        m_ref[i, h] = m
        a_ref[i, h] = acc

    @pl.when(c == 0)
    def _():
        m_ref[...] = jnp.full(m_ref.shape, NEG_INF, F32)
        a_ref[...] = jnp.zeros(a_ref.shape, F32)
        rr = lax.broadcasted_iota(jnp.int32, (t, t), 0)
        cc = lax.broadcasted_iota(jnp.int32, (t, t), 1)
        for h in range(nh):
            kb = kn_ref[pl.ds(h, t, stride=nh), :].astype(BF16)
            va = _with_ones(vn_ref[pl.ds(h, t, stride=nh), :].astype(BF16))
            if n_maps == 1:
                bias, visible = -f_ref[0, h, nc:nc + 1, 0:t], cc <= rr
            else:
                bias, visible = _diag_bias_and_mask(rr, cc, slope_ref[h][:, 0:1], p_len)
            for i, q in enumerate(head_queries(h)):
                s = lax.dot_general(q, kb, _NT, preferred_element_type=F32) + bias
                update(h, i, jnp.where(visible, s, NEG_INF), va)

    for h in range(nh):
        kb = kp_ref[pl.ds(h, ck, stride=nh), :].astype(BF16)
        va = _with_ones(vp_ref[pl.ds(h, ck, stride=nh), :].astype(BF16))
        if n_maps == 1:
            bias = -f_ref[0, h, pl.ds(c, 1), :]
        else:
            col = lax.broadcasted_iota(jnp.int32, (1, ck), 1) + (c * ck - p_len)
            bias = slope_ref[h][:, 0:1] * col.astype(F32)
        for i, q in enumerate(head_queries(h)):
            s = lax.dot_general(q, kb, _NT, preferred_element_type=F32) + bias
            update(h, i, s, va)

    @pl.when(c == nc - 1)
    def _():
        if n_maps == 2:
            lam = _lambda_full(lq1_ref[...], lk1_ref[...], lq2_ref[...], lk2_ref[...], lambda_init)
        for h in range(nh):
            cols = slice(h * HEAD_DIM, (h + 1) * HEAD_DIM)
            z = z_ref[0][:, cols]
            if n_maps == 1:
                o = _normalised(a_ref[0, h]) * _silu(z.astype(F32))
            else:
                o = _diff_finish(_normalised(a_ref[0, h]), _normalised(a_ref[1, h]), lam, g_ref[...], lambda_init, z)
            o_ref[0, :, cols] = o.astype(o_ref.dtype)


def _cached_attention_call(qz, k_past, v_past, k_new, v_new, *, q_col, z_col, frow=None, diff=None, name):
    b, t, _ = qz.shape
    nh = N_HEADS
    wf = nh * HEAD_DIM
    p_len = k_past.shape[0] // (b * nh)
    ck = min(ATT_TK_CACHE, p_len)
    nc = p_len // ck
    in_specs = [pl.BlockSpec((1, t, wf), lambda ib, c: (ib, 0, q_col)),
                pl.BlockSpec((ck * nh, HEAD_DIM), lambda ib, c: (ib * nc + c, 0)),
                pl.BlockSpec((ck * nh, HEAD_DIM), lambda ib, c: (ib * nc + c, 0)),
                pl.BlockSpec((t * nh, HEAD_DIM), lambda ib, c: (ib, 0)),
                pl.BlockSpec((t * nh, HEAD_DIM), lambda ib, c: (ib, 0))]
    z_spec = pl.BlockSpec((1, t, wf), lambda ib, c: (ib, 0, z_col))
    if diff is None:
        n_maps, lambda_init = 1, 0.0
        in_specs += [pl.BlockSpec((1, nh, nc + 1, ck), lambda ib, c: (ib, 0, 0, 0)), z_spec]
        args = (qz, k_past, v_past, k_new, v_new, frow, qz)
    else:
        n_maps = 2
        lam_params, subln_g, lambda_init = diff
        small = [p.reshape(1, -1) for p in lam_params] + [subln_g.reshape(1, -1)]
        in_specs += [z_spec, pl.BlockSpec((nh, 1, LANES), lambda ib, c: (0, 0, 0))]
        in_specs += [pl.BlockSpec(p.shape, lambda ib, c: (0, 0)) for p in small]
        args = (qz, k_past, v_past, k_new, v_new, qz, _alibi_slopes(), *small)
    return pl.pallas_call(
        functools.partial(_cached_attention_kernel, ck=ck, n_maps=n_maps, lambda_init=lambda_init),
        grid=(b, nc),
        in_specs=in_specs,
        out_specs=pl.BlockSpec((1, t, wf), lambda ib, c: (ib, 0, 0)),
        out_shape=jax.ShapeDtypeStruct((b, t, wf), BF16),
        scratch_shapes=[pltpu.VMEM((n_maps, nh, t, LANES), F32), pltpu.VMEM((n_maps, nh, t, 2 * HEAD_DIM), F32)],
        compiler_params=_cparams(2),
        name=name,
    )(*args)


def _s5_disc_kernel(are_ref, aim_ref, bre_ref, bim_ref, ldt_ref, abr_ref, abi_ref, bbr_ref, bbi_ref):
    a_re, a_im = are_ref[...], aim_ref[...]
    dt = jnp.exp(ldt_ref[...])
    mag = jnp.exp(dt * a_re)
    abar_re = mag * jnp.cos(dt * a_im)
    abar_im = mag * jnp.sin(dt * a_im)
    den = a_re * a_re + a_im * a_im
    nr = abar_re - 1.0
    ni = abar_im
    fr = (nr * a_re + ni * a_im) / den
    fi = (ni * a_re - nr * a_im) / den
    abr_ref[...] = abar_re
    abi_ref[...] = abar_im
    bbr_ref[...] = fr * bre_ref[...] - fi * bim_ref[...]
    bbi_ref[...] = fr * bim_ref[...] + fi * bre_ref[...]


def _s5_disc_call(a_re, a_im, b_re, b_im, log_dt):
    g, p, gc = b_re.shape
    flat = lambda a: jnp.broadcast_to(a, (g, p, gc)).reshape(-1, LANES)
    args = [flat(a_re[..., None]), flat(a_im[..., None]), flat(b_re), flat(b_im), flat(log_dt[:, None, None])]
    shape = jax.ShapeDtypeStruct(args[0].shape, F32)
    outs = pl.pallas_call(_s5_disc_kernel, out_shape=[shape] * 4, name="s5_discretise")(*args)
    abar_re, abar_im, bb_re, bb_im = [o.reshape(g, p, gc) for o in outs]
    return abar_re[..., 0], abar_im[..., 0], bb_re, bb_im


def _s5_scan_kernel(u_ref, bblk_ref, cblk_ref, ar_ref, ai_ref, d_ref, x0_ref, g_ref, st_ref, bu_ref, xs_ref,
                    *scratch, nb, lc, grouped):
    @pl.when(pl.program_id(1) == 0)
    def _():
        st_ref[...] = x0_ref[...]

    if grouped:
        utb_ref, gtb_ref = scratch
        for t in range(lc):
            src = (t // SUBLANES) * nb * SUBLANES + t % SUBLANES
            utb_ref[t * nb:(t + 1) * nb, :] = u_ref[pl.ds(src, nb, stride=SUBLANES), :].astype(BF16)
        u = utb_ref[...]
    else:
        u = u_ref[...]
    bu_ref[...] = jnp.dot(u, bblk_ref[0], preferred_element_type=F32)
    hs = STATE_HALF
    ar = jnp.broadcast_to(ar_ref[0], (nb, hs))
    ai = jnp.broadcast_to(ai_ref[0], (nb, hs))

    def step(t, carry):
        xr, xi = carry
        r0 = pl.multiple_of(t * nb, nb)
        nr = ar * xr - ai * xi + bu_ref[pl.ds(r0, nb), 0:hs]
        ni = ar * xi + ai * xr + bu_ref[pl.ds(r0, nb), hs:2 * hs]
        xs_ref[pl.ds(r0, nb), 0:hs] = nr.astype(BF16)
        xs_ref[pl.ds(r0, nb), hs:2 * hs] = ni.astype(BF16)
        return nr, ni

    xr, xi = lax.fori_loop(0, lc, step, (st_ref[:, 0:hs], st_ref[:, hs:2 * hs]))
    st_ref[:, 0:hs] = xr
    st_ref[:, hs:2 * hs] = xi
    y = jnp.dot(xs_ref[...], cblk_ref[0], preferred_element_type=F32) + d_ref[0] * u.astype(F32)
    g = jax.nn.gelu(y)
    if grouped:
        gtb_ref[...] = g
        for tg in range(lc // BF16_ROWS):
            for ib in range(nb):
                dst = (tg * nb + ib) * BF16_ROWS
                rows = gtb_ref[pl.ds(tg * BF16_ROWS * nb + ib, BF16_ROWS, stride=nb), :]
                g_ref[dst:dst + BF16_ROWS, :] = rows.astype(g_ref.dtype)
    else:
        g_ref[...] = g.astype(g_ref.dtype)


def _s5_scan_call(u2d, bblk, cblk, abar_re, abar_im, d_skip, x0, *, nb, lc, grouped):
    rows, d = u2d.shape
    nf = d // LANES
    hs = STATE_HALF
    chunk_rows = lc * nb
    scratch = [pltpu.VMEM((chunk_rows, 2 * hs), F32), pltpu.VMEM((chunk_rows, 2 * hs), BF16)]
    if grouped:
        scratch += [pltpu.VMEM((chunk_rows, LANES), BF16), pltpu.VMEM((chunk_rows, LANES), F32)]
    return pl.pallas_call(
        functools.partial(_s5_scan_kernel, nb=nb, lc=lc, grouped=grouped),
        grid=(nf, rows // chunk_rows),
        in_specs=[pl.BlockSpec((chunk_rows, LANES), lambda f, c: (c, f)),
                  pl.BlockSpec((1, LANES, 2 * hs), lambda f, c: (f, 0, 0)),
                  pl.BlockSpec((1, 2 * hs, LANES), lambda f, c: (f, 0, 0)),
                  pl.BlockSpec((1, 1, hs), lambda f, c: (f, 0, 0)),
                  pl.BlockSpec((1, 1, hs), lambda f, c: (f, 0, 0)),
                  pl.BlockSpec((1, 1, LANES), lambda f, c: (f, 0, 0)),
                  pl.BlockSpec((nb, 2 * hs), lambda f, c: (0, f))],
        out_specs=[pl.BlockSpec((chunk_rows, LANES), lambda f, c: (c, f)),
                   pl.BlockSpec((nb, 2 * hs), lambda f, c: (0, f))],
        out_shape=[jax.ShapeDtypeStruct((rows, d), BF16), jax.ShapeDtypeStruct((nb, nf * 2 * hs), F32)],
        scratch_shapes=scratch,
        compiler_params=_cparams(2),
        name="s5_scan",
    )(u2d, bblk, cblk, abar_re, abar_im, d_skip, x0)


def _s5_block_weights(abar_re, abar_im, bb_re, bb_im, c_re, c_im, d_skip):
    g = bb_re.shape[0]
    nf, gb = g // GROUPS_PER_BLOCK, GROUPS_PER_BLOCK
    eye = jnp.eye(gb, dtype=F32)
    b4 = lambda b: jnp.einsum("fgpc,gh->fgchp", b.reshape(nf, gb, P_STATE, GROUP_CH), eye).reshape(nf, LANES, STATE_HALF)
    c4 = lambda c: jnp.einsum("fgcp,gh->fhpgc", c.reshape(nf, gb, GROUP_CH, P_STATE), eye).reshape(nf, STATE_HALF, LANES)
    bblk = jnp.concatenate([b4(bb_re), b4(bb_im)], axis=-1).astype(BF16)
    cblk = jnp.concatenate([c4(c_re), -c4(c_im)], axis=1).astype(BF16)
    row = lambda a: a.reshape(nf, 1, STATE_HALF)
    return bblk, cblk, row(abar_re), row(abar_im), d_skip.reshape(nf, 1, LANES)


def _pack_state(s_re, s_im):
    b, g, p = s_re.shape
    nf = g // GROUPS_PER_BLOCK
    return jnp.concatenate([s_re.reshape(b, nf, STATE_HALF), s_im.reshape(b, nf, STATE_HALF)], axis=-1).reshape(b, -1)


def _unpack_state(st, g):
    b = st.shape[0]
    nf = g // GROUPS_PER_BLOCK
    st = st.reshape(b, nf, 2, GROUPS_PER_BLOCK, P_STATE)
    return st[:, :, 0].reshape(b, g, P_STATE), st[:, :, 1].reshape(b, g, P_STATE)


def _trunk(x, mods, cache, state, w):
    b, t, d = x.shape
    nh = N_HEADS
    fresh = cache is None

    shift, scale, gate = mods[0]
    h0 = _adaln_call(x, shift, scale, w["norm_g"][0])
    qz = _mm([(h0, w["w_qz"])], BF16, 4 * nh * HEAD_DIM, b=b, t=t, tn=1024, name="in_att_qz")
    kf, kf_b = _kv_proj(h0, w["w_kf"], with_bf16=fresh, name="in_att_kf")
    vf, vf_b = _kv_proj(h0, w["w_vf"], with_bf16=fresh, name="in_att_vf")
    kd, kd_b = _kv_proj(h0, w["w_kd"], with_bf16=fresh, name="in_att_kd")
    vd, vd_b = _kv_proj(h0, w["w_vd"], with_bf16=fresh, name="in_att_vd")
    logf = _mm([(h0, w["w_f"])], F32, LANES, b=b, t=t, tn=LANES, epilogue=_ep_logf,
               extras=[(w["b_forget"], "row")], name="in_att_logf")[..., :nh]
    lambda_init = 0.8 - 0.6 * math.exp(-0.3 * 0)
    diff = ((w["diff_lq1"], w["diff_lk1"], w["diff_lq2"], w["diff_lk2"]), w["diff_subln_g"], lambda_init)
    if fresh:
        frow = _cumsum_call(jnp.transpose(logf, (0, 2, 1)))
        o_f = _self_attention_call(qz, kf_b, vf_b, q_col=0, z_col=1, frow=frow, name="fox_attention_self")
        o_d = _self_attention_call(qz, kd_b, vd_b, q_col=2, z_col=3, diff=diff, name="diff_attention_self")
    else:
        fk, fv, fl, dk, dv = cache
        p_len = fl.shape[1]
        ck = min(ATT_TK_CACHE, p_len)
        rows = lambda a: a.reshape(-1, HEAD_DIM)
        lf = jnp.concatenate([fl.astype(F32), logf], axis=1)
        lf = jnp.pad(lf, ((0, 0), (0, (-lf.shape[1]) % ck), (0, 0)))
        frow = _cumsum_call(jnp.transpose(lf, (0, 2, 1))).reshape(b, nh, -1, ck)
        o_f = _cached_attention_call(qz, rows(fk), rows(fv), kf, vf, q_col=0, z_col=1, frow=frow,
                                     name="fox_attention_cached")
        o_d = _cached_attention_call(qz, rows(dk), rows(dv), kd, vd, q_col=2, z_col=3, diff=diff,
                                     name="diff_attention_cached")
    x1 = _mm([(o_f, w["w_out_f"]), (o_d, w["w_out_d"])], F32, d, b=b, t=t, tn=1024, epilogue=_ep_residual,
             extras=[(x, "tile"), (gate, "batch_row")], name="out_att")

    shift, scale, gate = mods[1]
    h1 = _adaln_call(x1, shift, scale, w["norm_g"][1])
    z = _mm([(h1, w["w_z"])], BF16, d, b=b, t=t, tn=1024, name="in_ssm_z")
    x0 = jnp.zeros((b, w["n_groups"] * 2 * P_STATE), F32) if state is None else _pack_state(*state)
    s5 = functools.partial(_s5_scan_call, bblk=w["bblk"], cblk=w["cblk"], abar_re=w["abar_re"],
                           abar_im=w["abar_im"], d_skip=w["d_skip"], x0=x0, nb=b)
    if t >= MM_ROWS:
        u = _mm([(h1, w["w_u"])], F32, d, b=b, t=t, tn=1024, out_grouped=SUBLANES, name="in_ssm_u")
        g2d, st = s5(u.reshape(t * b, d), lc=CHUNK, grouped=True)
        g = g2d.reshape(t // BF16_ROWS, b, BF16_ROWS, d)
        a2 = _mm([(g, w["w_glu"])], BF16, d, b=b, t=t, tn=1024, a_grouped=BF16_ROWS, epilogue=_ep_glu,
                 extras=[(g, "tile_grouped"), (w["b_glu"], "row"), (z, "tile")], name="glu")
    else:
        u = _mm([(h1, w["w_u"])], BF16, d, b=b, t=t, tn=1024, name="in_ssm_u")
        g2d, st = s5(jnp.transpose(u, (1, 0, 2)).reshape(t * b, d), lc=t, grouped=False)
        g = jnp.transpose(g2d.reshape(t, b, d), (1, 0, 2))
        a2 = _mm([(g, w["w_glu"])], BF16, d, b=b, t=t, tn=1024, epilogue=_ep_glu,
                 extras=[(g, "tile"), (w["b_glu"], "row"), (z, "tile")], name="glu")
    y = _mm([(a2, w["w_out_ssm"])], F32, d, b=b, t=t, tn=d, rows=MM_ROWS // 2, epilogue=_ep_residual_norm,
            extras=[(x1, "tile"), (gate, "batch_row"), (w["final_norm_g"], "row")], name="out_ssm_norm")
    n_re, n_im = _unpack_state(st, w["n_groups"])
    lead = lambda a: a.reshape((1, b, t, nh, HEAD_DIM))
    return y, (lead(kf), lead(vf), logf.reshape(1, b, t, nh), lead(kd), lead(vd)), (n_re[None], n_im[None])


def kernel(x_prompt, x_sample, cache_fox_k, cache_fox_v, cache_fox_logf, cache_diff_k, cache_diff_v, state_ssm_re, state_ssm_im, c_prompt, c_sample, norm_g, w_mod, b_mod, w_in_att, b_forget, w_out_att, diff_lq1, diff_lk1, diff_lq2, diff_lk2, diff_subln_g, w_in_ssm, ssm_a_re, ssm_a_im, ssm_b_re, ssm_b_im, ssm_c_re, ssm_c_im, ssm_d, ssm_log_dt, w_glu, b_glu, w_out_ssm, final_norm_g):
    d = x_prompt.shape[-1]
    bp, bs = x_prompt.shape[0], x_sample.shape[0]
    nh = N_HEADS
    wf = nh * HEAD_DIM
    assert w_in_att.shape[0] == 1 and w_in_ssm.shape[0] == 1 and norm_g.shape[0] == 2
    assert wf == d // 2 and w_in_att.shape[-1] == 8 * wf + nh

    wa = w_in_att[0]
    offs = np.cumsum([0, wf, wf, wf, nh, wf, wf, wf, wf, wf])
    sec = lambda i: wa[:, offs[i]:offs[i + 1]]
    abar_re, abar_im, bb_re, bb_im = _s5_disc_call(ssm_a_re[0], ssm_a_im[0], ssm_b_re[0], ssm_b_im[0], ssm_log_dt[0])
    bblk, cblk, ar_row, ai_row, d_row = _s5_block_weights(abar_re, abar_im, bb_re, bb_im, ssm_c_re[0], ssm_c_im[0],
                                                        ssm_d[0])
    w = dict(
        norm_g=norm_g,
        w_qz=jnp.concatenate([sec(0), sec(4), sec(5), sec(8)], axis=1).astype(BF16),
        w_kf=sec(1).astype(BF16), w_vf=sec(2).astype(BF16), w_kd=sec(6).astype(BF16), w_vd=sec(7).astype(BF16),
        w_f=jnp.pad(sec(3), ((0, 0), (0, LANES - nh))).astype(BF16),
        b_forget=jnp.pad(b_forget, ((0, 0), (0, LANES - nh))),
        w_out_f=w_out_att[0, :wf].astype(BF16), w_out_d=w_out_att[0, wf:].astype(BF16),
        diff_lq1=diff_lq1, diff_lk1=diff_lk1, diff_lq2=diff_lq2, diff_lk2=diff_lk2, diff_subln_g=diff_subln_g,
        w_u=w_in_ssm[0, :, :d].astype(BF16), w_z=w_in_ssm[0, :, d:].astype(BF16),
        bblk=bblk, cblk=cblk, abar_re=ar_row, abar_im=ai_row, d_skip=d_row, n_groups=ssm_a_re.shape[1],
        w_glu=w_glu[0].astype(BF16), b_glu=b_glu, w_out_ssm=w_out_ssm[0].astype(BF16),
        final_norm_g=final_norm_g.reshape(1, d),
    )

    mod = _mod_call(jnp.concatenate([c_prompt, c_sample], axis=0), w_mod, b_mod)

    def mods(rows):
        return [tuple(mod[l, rows, i * d:(i + 1) * d][:, None, :] for i in range(3)) for l in range(2)]

    y_p, att_p, ssm_p = _trunk(x_prompt, mods(slice(0, bp)), None, None, w)
    cache = (cache_fox_k, cache_fox_v, cache_fox_logf[0], cache_diff_k, cache_diff_v)
    y_s, att_s, ssm_s = _trunk(x_sample, mods(slice(bp, bp + bs)), cache, (state_ssm_re[0], state_ssm_im[0]), w)
    return (y_p, y_s) + att_p + ssm_p + att_s + ssm_s
```

```python
import functools
import math

import numpy as np
import jax
import jax.numpy as jnp
from jax import lax
from jax.experimental import pallas as pl
from jax.experimental.pallas import tpu as pltpu

F32 = jnp.float32
BF16 = jnp.bfloat16

EPS = 1e-6
NEG_INF = -1e30
CHUNK = 64
ALIBI_MAX_EXP = 8.0
N_HEADS = 8
HEAD_DIM = 128
GROUP_CH = 16
P_STATE = 64
LANES = 128
SUBLANES = 8
BF16_ROWS = 16
GROUPS_PER_BLOCK = LANES // GROUP_CH
STATE_HALF = GROUPS_PER_BLOCK * P_STATE
VMEM_LIMIT_BYTES = 56 * 1024 * 1024
MM_ROWS = 1024
ATT_TQ = 512
ATT_SUB = 256
ATT_TK_CACHE = 512


def _cparams(n_axes):
    return pltpu.CompilerParams(dimension_semantics=("arbitrary",) * n_axes,
                                vmem_limit_bytes=VMEM_LIMIT_BYTES)


def _silu(x):
    return x * jax.nn.sigmoid(x)


def _log_sigmoid(x):
    return jnp.minimum(x, 0.0) - jnp.log(1.0 + jnp.exp(-jnp.abs(x)))


def _mod_kernel(c_ref, w_ref, b_ref, o_ref):
    a = _silu(c_ref[...]).astype(BF16)
    acc = jnp.dot(a, w_ref[0].astype(BF16), preferred_element_type=F32)
    o_ref[0] = acc + b_ref[0]


def _mod_call(c_all, w_mod, b_mod, tn=512):
    depth, d, n = w_mod.shape
    r = c_all.shape[0]
    return pl.pallas_call(
        _mod_kernel,
        grid=(depth, n // tn),
        in_specs=[pl.BlockSpec((r, d), lambda l, j: (0, 0)),
                  pl.BlockSpec((1, d, tn), lambda l, j: (l, 0, j)),
                  pl.BlockSpec((1, 1, tn), lambda l, j: (l, 0, j))],
        out_specs=pl.BlockSpec((1, r, tn), lambda l, j: (l, 0, j)),
        out_shape=jax.ShapeDtypeStruct((depth, r, n), F32),
        compiler_params=_cparams(2),
        name="adaln_mod",
    )(c_all, w_mod, b_mod.reshape(depth, 1, n))


def _adaln_kernel(x_ref, shift_ref, scale_ref, g_ref, h_ref):
    x = x_ref[...]
    y = x * lax.rsqrt(jnp.mean(x * x, axis=-1, keepdims=True) + EPS) * g_ref[...]
    h_ref[...] = (y * (1.0 + scale_ref[...]) + shift_ref[...]).astype(h_ref.dtype)


def _row_tiles(b, t, rows):
    if t >= rows:
        return 1, rows
    return min(b, rows // t), t


def _adaln_call(x, shift, scale, g):
    b, t, d = x.shape
    bb, tt = _row_tiles(b, t, MM_ROWS // 2)
    return pl.pallas_call(
        _adaln_kernel,
        grid=(b // bb, t // tt),
        in_specs=[pl.BlockSpec((bb, tt, d), lambda i, j: (i, j, 0)),
                  pl.BlockSpec((bb, 1, d), lambda i, j: (i, 0, 0)),
                  pl.BlockSpec((bb, 1, d), lambda i, j: (i, 0, 0)),
                  pl.BlockSpec((1, d), lambda i, j: (0, 0))],
        out_specs=pl.BlockSpec((bb, tt, d), lambda i, j: (i, j, 0)),
        out_shape=jax.ShapeDtypeStruct((b, t, d), BF16),
        compiler_params=_cparams(2),
        name="adaln_norm",
    )(x, shift, scale, g.reshape(1, d))


def _mm_kernel(*refs, n_pairs, epilogue):
    o_ref = refs[-1]
    extra = refs[2 * n_pairs:-1]
    acc = None
    for p in range(n_pairs):
        a_ref, w_ref = refs[2 * p], refs[2 * p + 1]
        a = a_ref[...].reshape(-1, a_ref.shape[-1])
        part = jnp.dot(a, w_ref[...], preferred_element_type=F32)
        acc = part if acc is None else acc + part
    acc = acc.reshape(o_ref.shape)
    if epilogue is not None:
        acc = epilogue(acc, *[e[...] for e in extra])
    o_ref[...] = acc.astype(o_ref.dtype)


def _mm(pairs, out_dtype, n_out, *, b, t, tn, rows=None, a_grouped=0, out_grouped=0,
        epilogue=None, extras=(), name="matmul"):
    bb, tt = _row_tiles(b, t, MM_ROWS if rows is None else rows)
    if a_grouped or out_grouped:
        assert bb == 1
    nj = n_out // tn
    in_specs, args = [], []
    for a, w in pairs:
        k = w.shape[0]
        if a_grouped:
            in_specs.append(pl.BlockSpec((tt // a_grouped, 1, a_grouped, k), lambda ib, it, j: (it, ib, 0, 0)))
        else:
            in_specs.append(pl.BlockSpec((bb, tt, k), lambda ib, it, j: (ib, it, 0)))
        in_specs.append(pl.BlockSpec((k, tn), lambda ib, it, j: (0, j)))
        args += [a, w]
    for arr, kind in extras:
        if kind == "row":
            in_specs.append(pl.BlockSpec((1, tn), lambda ib, it, j: (0, j)))
        elif kind == "batch_row":
            in_specs.append(pl.BlockSpec((bb, 1, tn), lambda ib, it, j: (ib, 0, j)))
        elif kind == "tile":
            in_specs.append(pl.BlockSpec((bb, tt, tn), lambda ib, it, j: (ib, it, j)))
        elif kind == "tile_grouped":
            in_specs.append(pl.BlockSpec((tt // a_grouped, 1, a_grouped, tn), lambda ib, it, j: (it, ib, 0, j)))
        else:
            raise ValueError(kind)
        args.append(arr)
    if out_grouped:
        out_spec = pl.BlockSpec((tt // out_grouped, 1, out_grouped, tn), lambda ib, it, j: (it, ib, 0, j))
        out_shape = jax.ShapeDtypeStruct((t // out_grouped, b, out_grouped, n_out), out_dtype)
    else:
        out_spec = pl.BlockSpec((bb, tt, tn), lambda ib, it, j: (ib, it, j))
        out_shape = jax.ShapeDtypeStruct((b, t, n_out), out_dtype)
    return pl.pallas_call(
        functools.partial(_mm_kernel, n_pairs=len(pairs), epilogue=epilogue),
        grid=(b // bb, t // tt, nj),
        in_specs=in_specs,
        out_specs=out_spec,
        out_shape=out_shape,
        compiler_params=_cparams(3),
        name=name,
    )(*args)


def _ep_logf(acc, bias):
    return _log_sigmoid(acc + bias)


def _ep_residual(acc, x, gate):
    return x + gate * acc


def _ep_glu(acc, g, bias, z):
    g = g.reshape(acc.shape).astype(F32)
    return g * jax.nn.sigmoid(acc + bias) * _silu(z.astype(F32))


def _ep_residual_norm(acc, x, gate, fg):
    x2 = x + gate * acc
    return x2 * lax.rsqrt(jnp.mean(x2 * x2, axis=-1, keepdims=True) + EPS) * fg


def _kv_kernel(a_ref, w_ref, o_ref, *maybe_ob):
    a = a_ref[...].reshape(-1, a_ref.shape[-1])
    acc = jnp.dot(a, w_ref[...], preferred_element_type=F32)
    rows = acc.shape[0]
    for h in range(N_HEADS):
        part = acc[:, h * HEAD_DIM:(h + 1) * HEAD_DIM]
        o_ref[pl.ds(h, rows, stride=N_HEADS), :] = part
        if maybe_ob:
            ob_ref = maybe_ob[0]
            ob_ref[:, h] = part.reshape(ob_ref.shape[0], ob_ref.shape[2], HEAD_DIM).astype(ob_ref.dtype)


def _kv_proj(a, w, *, with_bf16, name):
    b, t, k = a.shape
    bb, tt = _row_tiles(b, t, MM_ROWS)
    n = N_HEADS * HEAD_DIM
    nt = t // tt
    out_shape = [jax.ShapeDtypeStruct((b * t * N_HEADS, HEAD_DIM), F32)]
    out_specs = [pl.BlockSpec((bb * tt * N_HEADS, HEAD_DIM), lambda ib, it: (ib * nt + it, 0))]
    if with_bf16:
        out_shape.append(jax.ShapeDtypeStruct((b, N_HEADS, t, HEAD_DIM), BF16))
        out_specs.append(pl.BlockSpec((bb, N_HEADS, tt, HEAD_DIM), lambda ib, it: (ib, 0, it, 0)))
    outs = pl.pallas_call(
        _kv_kernel,
        grid=(b // bb, nt),
        in_specs=[pl.BlockSpec((bb, tt, k), lambda ib, it: (ib, it, 0)),
                  pl.BlockSpec((k, n), lambda ib, it: (0, 0))],
        out_specs=out_specs,
        out_shape=out_shape,
        compiler_params=_cparams(2),
        name=name,
    )(a, w)
    return outs if with_bf16 else (outs[0], None)


def _cumsum_kernel(x_ref, o_ref):
    x = x_ref[0]
    n = x.shape[-1]
    lane = lax.broadcasted_iota(jnp.int32, x.shape, 1)
    d = 1
    while d < n:
        x = x + jnp.where(lane >= d, pltpu.roll(x, d, axis=1), 0.0)
        d *= 2
    o_ref[0] = x


def _cumsum_call(lf):
    b, nh, n = lf.shape
    return pl.pallas_call(
        _cumsum_kernel,
        grid=(b,),
        in_specs=[pl.BlockSpec((1, nh, n), lambda i: (i, 0, 0))],
        out_specs=pl.BlockSpec((1, nh, n), lambda i: (i, 0, 0)),
        out_shape=jax.ShapeDtypeStruct((b, nh, n), F32),
        compiler_params=_cparams(1),
        name="logf_cumsum",
    )(lf)


_NT = (((1,), (1,)), ((), ()))


def _lane_fit(x, width):
    if width <= LANES:
        return x[:, 0:width]
    return jnp.concatenate([x] * (width // LANES), axis=1)


def _softmax_update(s, va, m_prev, acc_prev):
    m_new = jnp.maximum(m_prev, jnp.max(s, axis=1, keepdims=True))
    alpha = jnp.exp(m_prev - m_new)
    p = jnp.exp(s - _lane_fit(m_new, s.shape[1]))
    acc = _lane_fit(alpha, 2 * LANES) * acc_prev + jnp.dot(p.astype(BF16), va, preferred_element_type=F32)
    return m_new, acc


def _ref_update(rows, s, va, m_ref, acc_ref):
    m, acc = _softmax_update(s, va, m_ref[rows, :], acc_ref[rows, :])
    m_ref[rows, :] = m
    acc_ref[rows, :] = acc


def _normalised(acc):
    return acc[:, 0:HEAD_DIM] / acc[:, HEAD_DIM:2 * HEAD_DIM]


def _with_ones(v):
    return jnp.concatenate([v, jnp.ones(v.shape, v.dtype)], axis=1)


def _lambda_full(lq1, lk1, lq2, lk2, lambda_init):
    s1 = jnp.sum(lq1 * lk1, axis=-1, keepdims=True)
    s2 = jnp.sum(lq2 * lk2, axis=-1, keepdims=True)
    return jnp.exp(s1) - jnp.exp(s2) + lambda_init


def _split_q_diff(q):
    qf = q.astype(F32) * ((HEAD_DIM // 2) ** -0.5)
    lane = lax.broadcasted_iota(jnp.int32, qf.shape, 1)
    first = lane < (HEAD_DIM // 2)
    return jnp.where(first, qf, 0.0).astype(BF16), jnp.where(first, 0.0, qf).astype(BF16)


def _diff_finish(o1, o2, lam, subln_g, lambda_init, z):
    o = o1 - lam * o2
    o = o * lax.rsqrt(jnp.mean(o * o, axis=-1, keepdims=True) + EPS) * subln_g
    o = o * (1.0 - lambda_init)
    return o * _silu(z.astype(F32))


def _alibi_slopes():
    s = np.power(2.0, -ALIBI_MAX_EXP * np.arange(1, N_HEADS + 1, dtype=np.float64) / N_HEADS)
    return jnp.asarray(np.broadcast_to(s[:, None, None], (N_HEADS, 1, LANES)), F32)


def _diag_bias_and_mask(rr, cc, slope, offset):
    visible = ((cc + offset) // CHUNK) <= ((rr + offset) // CHUNK)
    return slope * (rr - jnp.abs(rr - cc)).astype(F32), visible


def _self_attention_kernel(*refs, tq, sub, n_maps, lambda_init):
    if n_maps == 1:
        q_ref, k_ref, v_ref, f_ref, z_ref, o_ref, va_ref, m1_ref, a1_ref = refs
        stats = ((m1_ref, a1_ref),)
    else:
        (q_ref, k_ref, v_ref, z_ref, slope_ref, lq1_ref, lk1_ref, lq2_ref, lk2_ref, g_ref, o_ref,
         va_ref, m1_ref, a1_ref, m2_ref, a2_ref) = refs
        stats = ((m1_ref, a1_ref), (m2_ref, a2_ref))
        slope = slope_ref[0][:, 0:1]
    t = k_ref.shape[2]
    va_ref[:, 0:HEAD_DIM] = v_ref[0, 0]
    va_ref[:, HEAD_DIM:2 * HEAD_DIM] = jnp.ones((t, HEAD_DIM), BF16)
    for m_ref, a_ref in stats:
        m_ref[...] = jnp.full(m_ref.shape, NEG_INF, F32)
        a_ref[...] = jnp.zeros(a_ref.shape, F32)
    nsub = tq // sub

    for q_tile in range(t // tq):
        q0 = q_tile * tq
        q_rows = q_ref[0, q0:q0 + tq, :]
        qs = ((q_rows.astype(F32) * (HEAD_DIM ** -0.5)).astype(BF16),) if n_maps == 1 else _split_q_diff(q_rows)
        for j in range(q_tile):
            start = j * tq
            k = k_ref[0, 0, start:start + tq, :]
            va = va_ref[start:start + tq, :]
            if n_maps == 1:
                bias = -f_ref[0, j:j + 1, :]
            else:
                col = lax.broadcasted_iota(jnp.int32, (1, tq), 1) + (j - q_tile) * tq
                bias = slope * col.astype(F32)
            for r in range(nsub):
                rows = slice(q0 + r * sub, q0 + (r + 1) * sub)
                for q, (m_ref, a_ref) in zip(qs, stats):
                    s = lax.dot_general(q[r * sub:(r + 1) * sub], k, _NT, preferred_element_type=F32) + bias
                    _ref_update(rows, s, va, m_ref, a_ref)
        for r in range(nsub):
            rows = slice(q0 + r * sub, q0 + (r + 1) * sub)
            w = (r + 1) * sub
            k = k_ref[0, 0, q0:q0 + w, :]
            va = va_ref[q0:q0 + w, :]
            rr = lax.broadcasted_iota(jnp.int32, (sub, w), 0) + r * sub
            cc = lax.broadcasted_iota(jnp.int32, (sub, w), 1)
            if n_maps == 1:
                bias, visible = -f_ref[0, q_tile:q_tile + 1, 0:w], cc <= rr
            else:
                bias, visible = _diag_bias_and_mask(rr, cc, slope, 0)
            for q, (m_ref, a_ref) in zip(qs, stats):
                s = lax.dot_general(q[r * sub:(r + 1) * sub], k, _NT, preferred_element_type=F32) + bias
                _ref_update(rows, jnp.where(visible, s, NEG_INF), va, m_ref, a_ref)

    if n_maps == 1:
        o = _normalised(a1_ref[...]) * _silu(z_ref[0].astype(F32))
    else:
        lam = _lambda_full(lq1_ref[...], lk1_ref[...], lq2_ref[...], lk2_ref[...], lambda_init)
        o = _diff_finish(_normalised(a1_ref[...]), _normalised(a2_ref[...]), lam, g_ref[...], lambda_init, z_ref[0])
    o_ref[0] = o.astype(o_ref.dtype)


def _self_attention_call(qz, kb, vb, *, q_col, z_col, frow=None, diff=None, name):
    b, t, _ = qz.shape
    tq = min(ATT_TQ, t)
    sub = min(ATT_SUB, tq)
    nh = N_HEADS
    in_specs = [pl.BlockSpec((1, t, HEAD_DIM), lambda ib, h: (ib, 0, q_col * nh + h)),
                pl.BlockSpec((1, 1, t, HEAD_DIM), lambda ib, h: (ib, h, 0, 0)),
                pl.BlockSpec((1, 1, t, HEAD_DIM), lambda ib, h: (ib, h, 0, 0))]
    z_spec = pl.BlockSpec((1, t, HEAD_DIM), lambda ib, h: (ib, 0, z_col * nh + h))
    stats = [pltpu.VMEM((t, LANES), F32), pltpu.VMEM((t, 2 * HEAD_DIM), F32)]
    if diff is None:
        n_maps, lambda_init = 1, 0.0
        in_specs += [pl.BlockSpec((1, t // tq, tq), lambda ib, h: (ib * nh + h, 0, 0)), z_spec]
        args = (qz, kb, vb, frow.reshape(b * nh, t // tq, tq), qz)
    else:
        n_maps = 2
        lam_params, subln_g, lambda_init = diff
        small = [p.reshape(1, -1) for p in lam_params] + [subln_g.reshape(1, -1)]
        in_specs += [z_spec, pl.BlockSpec((1, 1, LANES), lambda ib, h: (h, 0, 0))]
        in_specs += [pl.BlockSpec(p.shape, lambda ib, h: (0, 0)) for p in small]
        args = (qz, kb, vb, qz, _alibi_slopes(), *small)
    return pl.pallas_call(
        functools.partial(_self_attention_kernel, tq=tq, sub=sub, n_maps=n_maps, lambda_init=lambda_init),
        grid=(b, nh),
        in_specs=in_specs,
        out_specs=pl.BlockSpec((1, t, HEAD_DIM), lambda ib, h: (ib, 0, h)),
        out_shape=jax.ShapeDtypeStruct((b, t, nh * HEAD_DIM), BF16),
        scratch_shapes=[pltpu.VMEM((t, 2 * HEAD_DIM), BF16)] + stats * n_maps,
        compiler_params=_cparams(2),
        name=name,
    )(*args)


def _cached_attention_kernel(*refs, ck, n_maps, lambda_init):
    if n_maps == 1:
        q_ref, kp_ref, vp_ref, kn_ref, vn_ref, f_ref, z_ref, o_ref, m_ref, a_ref = refs
    else:
        (q_ref, kp_ref, vp_ref, kn_ref, vn_ref, z_ref, slope_ref, lq1_ref, lk1_ref, lq2_ref, lk2_ref, g_ref,
         o_ref, m_ref, a_ref) = refs
    c = pl.program_id(1)
    nc = pl.num_programs(1)
    t = q_ref.shape[1]
    p_len = nc * ck
    nh = N_HEADS

    def head_queries(h):
        q = q_ref[0][:, h * HEAD_DIM:(h + 1) * HEAD_DIM]
        if n_maps == 1:
            return ((q.astype(F32) * (HEAD_DIM ** -0.5)).astype(BF16),)
        return _split_q_diff(q)

    def update(h, i, s, va):
        m, acc = _softmax_update(s, va, m_ref[i, h], a_ref[i, h])
        m_ref[i, h] = m
        a_ref[i, h] = acc

    @pl.when(c == 0)
    def _():
        m_ref[...] = jnp.full(m_ref.shape, NEG_INF, F32)
        a_ref[...] = jnp.zeros(a_ref.shape, F32)
        rr = lax.broadcasted_iota(jnp.int32, (t, t), 0)
        cc = lax.broadcasted_iota(jnp.int32, (t, t), 1)
        for h in range(nh):
            kb = kn_ref[pl.ds(h, t, stride=nh), :].astype(BF16)
            va = _with_ones(vn_ref[pl.ds(h, t, stride=nh), :].astype(BF16))
            if n_maps == 1:
                bias, visible = -f_ref[0, h, nc:nc + 1, 0:t], cc <= rr
            else:
                bias, visible = _diag_bias_and_mask(rr, cc, slope_ref[h][:, 0:1], p_len)
            for i, q in enumerate(head_queries(h)):
                s = lax.dot_general(q, kb, _NT, preferred_element_type=F32) + bias
                update(h, i, jnp.where(visible, s, NEG_INF), va)

    for h in range(nh):
        kb = kp_ref[pl.ds(h, ck, stride=nh), :].astype(BF16)
        va = _with_ones(vp_ref[pl.ds(h, ck, stride=nh), :].astype(BF16))
        if n_maps == 1:
            bias = -f_ref[0, h, pl.ds(c, 1), :]
        else:
            col = lax.broadcasted_iota(jnp.int32, (1, ck), 1) + (c * ck - p_len)
            bias = slope_ref[h][:, 0:1] * col.astype(F32)
        for i, q in enumerate(head_queries(h)):
            s = lax.dot_general(q, kb, _NT, preferred_element_type=F32) + bias
            update(h, i, s, va)

    @pl.when(c == nc - 1)
    def _():
        if n_maps == 2:
            lam = _lambda_full(lq1_ref[...], lk1_ref[...], lq2_ref[...], lk2_ref[...], lambda_init)
        for h in range(nh):
            cols = slice(h * HEAD_DIM, (h + 1) * HEAD_DIM)
            z = z_ref[0][:, cols]
            if n_maps == 1:
                o = _normalised(a_ref[0, h]) * _silu(z.astype(F32))
            else:
                o = _diff_finish(_normalised(a_ref[0, h]), _normalised(a_ref[1, h]), lam, g_ref[...], lambda_init, z)
            o_ref[0, :, cols] = o.astype(o_ref.dtype)


def _cached_attention_call(qz, k_past, v_past, k_new, v_new, *, q_col, z_col, frow=None, diff=None, name):
    b, t, _ = qz.shape
    nh = N_HEADS
    wf = nh * HEAD_DIM
    p_len = k_past.shape[0] // (b * nh)
    ck = min(ATT_TK_CACHE, p_len)
    nc = p_len // ck
    in_specs = [pl.BlockSpec((1, t, wf), lambda ib, c: (ib, 0, q_col)),
                pl.BlockSpec((ck * nh, HEAD_DIM), lambda ib, c: (ib * nc + c, 0)),
                pl.BlockSpec((ck * nh, HEAD_DIM), lambda ib, c: (ib * nc + c, 0)),
                pl.BlockSpec((t * nh, HEAD_DIM), lambda ib, c: (ib, 0)),
                pl.BlockSpec((t * nh, HEAD_DIM), lambda ib, c: (ib, 0))]
    z_spec = pl.BlockSpec((1, t, wf), lambda ib, c: (ib, 0, z_col))
    if diff is None:
        n_maps, lambda_init = 1, 0.0
        in_specs += [pl.BlockSpec((1, nh, nc + 1, ck), lambda ib, c: (ib, 0, 0, 0)), z_spec]
        args = (qz, k_past, v_past, k_new, v_new, frow, qz)
    else:
        n_maps = 2
        lam_params, subln_g, lambda_init = diff
        small = [p.reshape(1, -1) for p in lam_params] + [subln_g.reshape(1, -1)]
        in_specs += [z_spec, pl.BlockSpec((nh, 1, LANES), lambda ib, c: (0, 0, 0))]
        in_specs += [pl.BlockSpec(p.shape, lambda ib, c: (0, 0)) for p in small]
        args = (qz, k_past, v_past, k_new, v_new, qz, _alibi_slopes(), *small)
    return pl.pallas_call(
        functools.partial(_cached_attention_kernel, ck=ck, n_maps=n_maps, lambda_init=lambda_init),
        grid=(b, nc),
        in_specs=in_specs,
        out_specs=pl.BlockSpec((1, t, wf), lambda ib, c: (ib, 0, 0)),
        out_shape=jax.ShapeDtypeStruct((b, t, wf), BF16),
        scratch_shapes=[pltpu.VMEM((n_maps, nh, t, LANES), F32), pltpu.VMEM((n_maps, nh, t, 2 * HEAD_DIM), F32)],
        compiler_params=_cparams(2),
        name=name,
    )(*args)


def _s5_disc_kernel(are_ref, aim_ref, bre_ref, bim_ref, ldt_ref, abr_ref, abi_ref, bbr_ref, bbi_ref):
    a_re, a_im = are_ref[...], aim_ref[...]
    dt = jnp.exp(ldt_ref[...])
    mag = jnp.exp(dt * a_re)
    abar_re = mag * jnp.cos(dt * a_im)
    abar_im = mag * jnp.sin(dt * a_im)
    den = a_re * a_re + a_im * a_im
    nr = abar_re - 1.0
    ni = abar_im
    fr = (nr * a_re + ni * a_im) / den
    fi = (ni * a_re - nr * a_im) / den
    abr_ref[...] = abar_re
    abi_ref[...] = abar_im
    bbr_ref[...] = fr * bre_ref[...] - fi * bim_ref[...]
    bbi_ref[...] = fr * bim_ref[...] + fi * bre_ref[...]


def _s5_disc_call(a_re, a_im, b_re, b_im, log_dt):
    g, p, gc = b_re.shape
    flat = lambda a: jnp.broadcast_to(a, (g, p, gc)).reshape(-1, LANES)
    args = [flat(a_re[..., None]), flat(a_im[..., None]), flat(b_re), flat(b_im), flat(log_dt[:, None, None])]
    shape = jax.ShapeDtypeStruct(args[0].shape, F32)
    outs = pl.pallas_call(_s5_disc_kernel, out_shape=[shape] * 4, name="s5_discretise")(*args)
    abar_re, abar_im, bb_re, bb_im = [o.reshape(g, p, gc) for o in outs]
    return abar_re[..., 0], abar_im[..., 0], bb_re, bb_im


def _s5_scan_kernel(u_ref, bblk_ref, cblk_ref, ar_ref, ai_ref, d_ref, x0_ref, g_ref, st_ref, bu_ref, xs_ref,
                    *scratch, nb, lc, grouped):
    @pl.when(pl.program_id(1) == 0)
    def _():
        st_ref[...] = x0_ref[...]

    if grouped:
        utb_ref, gtb_ref = scratch
        for t in range(lc):
            src = (t // SUBLANES) * nb * SUBLANES + t % SUBLANES
            utb_ref[t * nb:(t + 1) * nb, :] = u_ref[pl.ds(src, nb, stride=SUBLANES), :].astype(BF16)
        u_tb = utb_ref
    else:
        u_tb = u_ref
    hs = STATE_HALF
    gr = BF16_ROWS * nb

    def project(k):
        rows = slice(k * gr, (k + 1) * gr)
        bu_ref[rows, :] = jnp.dot(u_tb[rows, :], bblk_ref[0], preferred_element_type=F32)

    ar = jnp.broadcast_to(ar_ref[0], (nb, hs))
    ai = jnp.broadcast_to(ai_ref[0], (nb, hs))
    xr, xi = st_ref[:, 0:hs], st_ref[:, hs:2 * hs]
    project(0)
    for k in range(lc // BF16_ROWS):
        if (k + 1) * BF16_ROWS < lc:
            project(k + 1)
        for t in range(k * BF16_ROWS, (k + 1) * BF16_ROWS):
            tok = slice(t * nb, (t + 1) * nb)
            nr = ar * xr - ai * xi + bu_ref[tok, 0:hs]
            ni = ar * xi + ai * xr + bu_ref[tok, hs:2 * hs]
            xs_ref[tok, 0:hs] = nr.astype(BF16)
            xs_ref[tok, hs:2 * hs] = ni.astype(BF16)
            xr, xi = nr, ni
        rows = slice(k * gr, (k + 1) * gr)
        y = jnp.dot(xs_ref[rows, :], cblk_ref[0], preferred_element_type=F32) + d_ref[0] * u_tb[rows, :].astype(F32)
        g = jax.nn.gelu(y)
        if grouped:
            gtb_ref[rows, :] = g
            for ib in range(nb):
                dst = (k * nb + ib) * BF16_ROWS
                g_ref[dst:dst + BF16_ROWS, :] = gtb_ref[pl.ds(k * gr + ib, BF16_ROWS, stride=nb), :].astype(g_ref.dtype)
        else:
            g_ref[rows, :] = g.astype(g_ref.dtype)
    st_ref[:, 0:hs] = xr
    st_ref[:, hs:2 * hs] = xi


def _s5_scan_call(u2d, bblk, cblk, abar_re, abar_im, d_skip, x0, *, nb, lc, grouped):
    rows, d = u2d.shape
    nf = d // LANES
    hs = STATE_HALF
    chunk_rows = lc * nb
    scratch = [pltpu.VMEM((chunk_rows, 2 * hs), F32), pltpu.VMEM((chunk_rows, 2 * hs), BF16)]
    if grouped:
        scratch += [pltpu.VMEM((chunk_rows, LANES), BF16), pltpu.VMEM((chunk_rows, LANES), F32)]
    return pl.pallas_call(
        functools.partial(_s5_scan_kernel, nb=nb, lc=lc, grouped=grouped),
        grid=(nf, rows // chunk_rows),
        in_specs=[pl.BlockSpec((chunk_rows, LANES), lambda f, c: (c, f)),
                  pl.BlockSpec((1, LANES, 2 * hs), lambda f, c: (f, 0, 0)),
                  pl.BlockSpec((1, 2 * hs, LANES), lambda f, c: (f, 0, 0)),
                  pl.BlockSpec((1, 1, hs), lambda f, c: (f, 0, 0)),
                  pl.BlockSpec((1, 1, hs), lambda f, c: (f, 0, 0)),
                  pl.BlockSpec((1, 1, LANES), lambda f, c: (f, 0, 0)),
                  pl.BlockSpec((nb, 2 * hs), lambda f, c: (0, f))],
        out_specs=[pl.BlockSpec((chunk_rows, LANES), lambda f, c: (c, f)),
                   pl.BlockSpec((nb, 2 * hs), lambda f, c: (0, f))],
        out_shape=[jax.ShapeDtypeStruct((rows, d), BF16), jax.ShapeDtypeStruct((nb, nf * 2 * hs), F32)],
        scratch_shapes=scratch,
        compiler_params=_cparams(2),
        name="s5_scan",
    )(u2d, bblk, cblk, abar_re, abar_im, d_skip, x0)


def _s5_block_weights(abar_re, abar_im, bb_re, bb_im, c_re, c_im, d_skip):
    g = bb_re.shape[0]
    nf, gb = g // GROUPS_PER_BLOCK, GROUPS_PER_BLOCK
    eye = jnp.eye(gb, dtype=F32)
    b4 = lambda b: jnp.einsum("fgpc,gh->fgchp", b.reshape(nf, gb, P_STATE, GROUP_CH), eye).reshape(nf, LANES, STATE_HALF)
    c4 = lambda c: jnp.einsum("fgcp,gh->fhpgc", c.reshape(nf, gb, GROUP_CH, P_STATE), eye).reshape(nf, STATE_HALF, LANES)
    bblk = jnp.concatenate([b4(bb_re), b4(bb_im)], axis=-1).astype(BF16)
    cblk = jnp.concatenate([c4(c_re), -c4(c_im)], axis=1).astype(BF16)
    row = lambda a: a.reshape(nf, 1, STATE_HALF)
    return bblk, cblk, row(abar_re), row(abar_im), d_skip.reshape(nf, 1, LANES)


def _pack_state(s_re, s_im):
    b, g, p = s_re.shape
    nf = g // GROUPS_PER_BLOCK
    return jnp.concatenate([s_re.reshape(b, nf, STATE_HALF), s_im.reshape(b, nf, STATE_HALF)], axis=-1).reshape(b, -1)


def _unpack_state(st, g):
    b = st.shape[0]
    nf = g // GROUPS_PER_BLOCK
    st = st.reshape(b, nf, 2, GROUPS_PER_BLOCK, P_STATE)
    return st[:, :, 0].reshape(b, g, P_STATE), st[:, :, 1].reshape(b, g, P_STATE)


def _trunk(x, mods, cache, state, w):
    b, t, d = x.shape
    nh = N_HEADS
    fresh = cache is None

    shift, scale, gate = mods[0]
    h0 = _adaln_call(x, shift, scale, w["norm_g"][0])
    qz = _mm([(h0, w["w_qz"])], BF16, 4 * nh * HEAD_DIM, b=b, t=t, tn=1024, name="in_att_qz")
    kf, kf_b = _kv_proj(h0, w["w_kf"], with_bf16=fresh, name="in_att_kf")
    vf, vf_b = _kv_proj(h0, w["w_vf"], with_bf16=fresh, name="in_att_vf")
    kd, kd_b = _kv_proj(h0, w["w_kd"], with_bf16=fresh, name="in_att_kd")
    vd, vd_b = _kv_proj(h0, w["w_vd"], with_bf16=fresh, name="in_att_vd")
    logf = _mm([(h0, w["w_f"])], F32, LANES, b=b, t=t, tn=LANES, epilogue=_ep_logf,
               extras=[(w["b_forget"], "row")], name="in_att_logf")[..., :nh]
    lambda_init = 0.8 - 0.6 * math.exp(-0.3 * 0)
    diff = ((w["diff_lq1"], w["diff_lk1"], w["diff_lq2"], w["diff_lk2"]), w["diff_subln_g"], lambda_init)
    if fresh:
        frow = _cumsum_call(jnp.transpose(logf, (0, 2, 1)))
        o_f = _self_attention_call(qz, kf_b, vf_b, q_col=0, z_col=1, frow=frow, name="fox_attention_self")
        o_d = _self_attention_call(qz, kd_b, vd_b, q_col=2, z_col=3, diff=diff, name="diff_attention_self")
    else:
        fk, fv, fl, dk, dv = cache
        p_len = fl.shape[1]
        ck = min(ATT_TK_CACHE, p_len)
        rows = lambda a: a.reshape(-1, HEAD_DIM)
        lf = jnp.concatenate([fl.astype(F32), logf], axis=1)
        lf = jnp.pad(lf, ((0, 0), (0, (-lf.shape[1]) % ck), (0, 0)))
        frow = _cumsum_call(jnp.transpose(lf, (0, 2, 1))).reshape(b, nh, -1, ck)
        o_f = _cached_attention_call(qz, rows(fk), rows(fv), kf, vf, q_col=0, z_col=1, frow=frow,
                                     name="fox_attention_cached")
        o_d = _cached_attention_call(qz, rows(dk), rows(dv), kd, vd, q_col=2, z_col=3, diff=diff,
                                     name="diff_attention_cached")
    x1 = _mm([(o_f, w["w_out_f"]), (o_d, w["w_out_d"])], F32, d, b=b, t=t, tn=1024, epilogue=_ep_residual,
             extras=[(x, "tile"), (gate, "batch_row")], name="out_att")

    shift, scale, gate = mods[1]
    h1 = _adaln_call(x1, shift, scale, w["norm_g"][1])
    z = _mm([(h1, w["w_z"])], BF16, d, b=b, t=t, tn=1024, name="in_ssm_z")
    x0 = jnp.zeros((b, w["n_groups"] * 2 * P_STATE), F32) if state is None else _pack_state(*state)
    s5 = functools.partial(_s5_scan_call, bblk=w["bblk"], cblk=w["cblk"], abar_re=w["abar_re"],
                           abar_im=w["abar_im"], d_skip=w["d_skip"], x0=x0, nb=b)
    if t >= MM_ROWS:
        u = _mm([(h1, w["w_u"])], F32, d, b=b, t=t, tn=1024, out_grouped=SUBLANES, name="in_ssm_u")
        g2d, st = s5(u.reshape(t * b, d), lc=CHUNK, grouped=True)
        g = g2d.reshape(t // BF16_ROWS, b, BF16_ROWS, d)
        a2 = _mm([(g, w["w_glu"])], BF16, d, b=b, t=t, tn=1024, a_grouped=BF16_ROWS, epilogue=_ep_glu,
                 extras=[(g, "tile_grouped"), (w["b_glu"], "row"), (z, "tile")], name="glu")
    else:
        u = _mm([(h1, w["w_u"])], BF16, d, b=b, t=t, tn=1024, name="in_ssm_u")
        g2d, st = s5(jnp.transpose(u, (1, 0, 2)).reshape(t * b, d), lc=t, grouped=False)
        g = jnp.transpose(g2d.reshape(t, b, d), (1, 0, 2))
        a2 = _mm([(g, w["w_glu"])], BF16, d, b=b, t=t, tn=1024, epilogue=_ep_glu,
                 extras=[(g, "tile"), (w["b_glu"], "row"), (z, "tile")], name="glu")
    y = _mm([(a2, w["w_out_ssm"])], F32, d, b=b, t=t, tn=d, rows=MM_ROWS // 2, epilogue=_ep_residual_norm,
            extras=[(x1, "tile"), (gate, "batch_row"), (w["final_norm_g"], "row")], name="out_ssm_norm")
    n_re, n_im = _unpack_state(st, w["n_groups"])
    lead = lambda a: a.reshape((1, b, t, nh, HEAD_DIM))
    return y, (lead(kf), lead(vf), logf.reshape(1, b, t, nh), lead(kd), lead(vd)), (n_re[None], n_im[None])


def kernel(x_prompt, x_sample, cache_fox_k, cache_fox_v, cache_fox_logf, cache_diff_k, cache_diff_v, state_ssm_re, state_ssm_im, c_prompt, c_sample, norm_g, w_mod, b_mod, w_in_att, b_forget, w_out_att, diff_lq1, diff_lk1, diff_lq2, diff_lk2, diff_subln_g, w_in_ssm, ssm_a_re, ssm_a_im, ssm_b_re, ssm_b_im, ssm_c_re, ssm_c_im, ssm_d, ssm_log_dt, w_glu, b_glu, w_out_ssm, final_norm_g):
    d = x_prompt.shape[-1]
    bp, bs = x_prompt.shape[0], x_sample.shape[0]
    nh = N_HEADS
    wf = nh * HEAD_DIM
    assert w_in_att.shape[0] == 1 and w_in_ssm.shape[0] == 1 and norm_g.shape[0] == 2
    assert wf == d // 2 and w_in_att.shape[-1] == 8 * wf + nh

    wa = w_in_att[0]
    offs = np.cumsum([0, wf, wf, wf, nh, wf, wf, wf, wf, wf])
    sec = lambda i: wa[:, offs[i]:offs[i + 1]]
    abar_re, abar_im, bb_re, bb_im = _s5_disc_call(ssm_a_re[0], ssm_a_im[0], ssm_b_re[0], ssm_b_im[0], ssm_log_dt[0])
    bblk, cblk, ar_row, ai_row, d_row = _s5_block_weights(abar_re, abar_im, bb_re, bb_im, ssm_c_re[0], ssm_c_im[0],
                                                        ssm_d[0])
    w = dict(
        norm_g=norm_g,
        w_qz=jnp.concatenate([sec(0), sec(4), sec(5), sec(8)], axis=1).astype(BF16),
        w_kf=sec(1).astype(BF16), w_vf=sec(2).astype(BF16), w_kd=sec(6).astype(BF16), w_vd=sec(7).astype(BF16),
        w_f=jnp.pad(sec(3), ((0, 0), (0, LANES - nh))).astype(BF16),
        b_forget=jnp.pad(b_forget, ((0, 0), (0, LANES - nh))),
        w_out_f=w_out_att[0, :wf].astype(BF16), w_out_d=w_out_att[0, wf:].astype(BF16),
        diff_lq1=diff_lq1, diff_lk1=diff_lk1, diff_lq2=diff_lq2, diff_lk2=diff_lk2, diff_subln_g=diff_subln_g,
        w_u=w_in_ssm[0, :, :d].astype(BF16), w_z=w_in_ssm[0, :, d:].astype(BF16),
        bblk=bblk, cblk=cblk, abar_re=ar_row, abar_im=ai_row, d_skip=d_row, n_groups=ssm_a_re.shape[1],
        w_glu=w_glu[0].astype(BF16), b_glu=b_glu, w_out_ssm=w_out_ssm[0].astype(BF16),
        final_norm_g=final_norm_g.reshape(1, d),
    )

    mod = _mod_call(jnp.concatenate([c_prompt, c_sample], axis=0), w_mod, b_mod)

    def mods(rows):
        return [tuple(mod[l, rows, i * d:(i + 1) * d][:, None, :] for i in range(3)) for l in range(2)]

    y_p, att_p, ssm_p = _trunk(x_prompt, mods(slice(0, bp)), None, None, w)
    cache = (cache_fox_k, cache_fox_v, cache_fox_logf[0], cache_diff_k, cache_diff_v)
    y_s, att_s, ssm_s = _trunk(x_sample, mods(slice(bp, bp + bs)), cache, (state_ssm_re[0], state_ssm_im[0]), w)
    return (y_p, y_s) + att_p + ssm_p + att_s + ssm_s
```
